```python
import math
import jax
import jax.numpy as jnp
from jax import lax
import numpy as np

D_MODEL = 1024
BATCH = 8
SEQ = 2048
DEPTH = 2
DEC_BATCH = 128
DEC_SEQ = 8
PAST_LEN = 16384
PAGE_SIZE = 128

SSM_EXPAND = 2
SSM_INNER = SSM_EXPAND * D_MODEL
SSM_HEAD_DIM = 64
SSM_HEADS = SSM_INNER // SSM_HEAD_DIM
SSM_GROUPS = 4
SSM_HEADS_PER_GROUP = SSM_HEADS // SSM_GROUPS
SSM_STATE = 128
SSM_CONV = 4
SSM_CONV_DIM = SSM_INNER + 2 * SSM_GROUPS * SSM_STATE
SSM_CHUNK = 128
HG_DIM = D_MODEL
HG_HEAD_DIM = 128
HG_HEADS = HG_DIM // HG_HEAD_DIM
HG_CHUNK = 64
N_EXPERTS = 32
TOP_K = 4
D_FF = D_MODEL
SWIGLU_LIMIT = 7.0
SWIGLU_ALPHA = 1.702
EPS = 1e-6
IN_SIZES = (SSM_INNER, SSM_CONV_DIM, SSM_HEADS, HG_DIM, HG_DIM, HG_DIM, HG_DIM, D_MODEL, D_MODEL)
IN_DIM = SSM_INNER + SSM_CONV_DIM + SSM_HEADS + 4 * HG_DIM + 2 * D_MODEL

kernel_name = 'hybrid_ssd_hgrn2_moe_adaln_step'

F32 = jnp.float32


def _rmsnorm(x, w):
    xf = x.astype(F32)
    y = xf * lax.rsqrt(jnp.mean(xf * xf, axis=-1, keepdims=True) + EPS)
    return (y * w.astype(F32)).astype(x.dtype)


def _split_cols(p):
    offs, acc = [], 0
    for s in IN_SIZES[:-1]:
        acc += s
        offs.append(acc)
    return jnp.split(p, offs, axis=-1)


def _to_chunks(t, L):
    b, T = t.shape[:2]
    return jnp.moveaxis(t.reshape((b, T // L, L) + t.shape[2:]), 1, 0)


def _from_chunks(t):
    t = jnp.moveaxis(t, 0, 1)
    return t.reshape((t.shape[0], t.shape[1] * t.shape[2]) + t.shape[3:])


def _causal_conv(u, buf, w, b):
    T = u.shape[1]
    xp = jnp.concatenate([buf.astype(u.dtype), u], axis=1)
    out = lax.conv_general_dilated(xp, w[:, None, :].astype(u.dtype), window_strides=(1,), padding='VALID',
                                   dimension_numbers=('NWC', 'WIO', 'NWC'), feature_group_count=u.shape[-1])
    return out + b.astype(u.dtype), xp[:, T:]


def _ssd_scan(xdt, a, bm, cm, s0):
    b, T = xdt.shape[:2]
    L = math.gcd(T, SSM_CHUNK)
    G, HG, P, N = SSM_GROUPS, SSM_HEADS_PER_GROUP, SSM_HEAD_DIM, SSM_STATE
    xs = (_to_chunks(xdt.astype(F32).reshape(b, T, G, HG, P), L),
          _to_chunks(a.astype(F32).reshape(b, T, G, HG), L),
          _to_chunks(bm.astype(F32), L), _to_chunks(cm.astype(F32), L))
    causal = jnp.tril(jnp.ones((L, L), dtype=bool))[None, :, :, None, None]

    def step(s, inp):
        xc, ac, bc, cc = inp
        acs = jnp.cumsum(ac, axis=1)
        decay = jnp.exp(jnp.where(causal, acs[:, :, None] - acs[:, None, :], -jnp.inf))
        scores = jnp.einsum('bign,bjgn->bgij', cc, bc)
        y = jnp.einsum('bgij,bijgh,bjghp->bighp', scores, decay, xc)
        y = y + jnp.einsum('bign,bghpn->bighp', cc, s) * jnp.exp(acs)[..., None]
        last = acs[:, -1]
        wj = jnp.exp(last[:, None] - acs)
        s = jnp.exp(last)[..., None, None] * s + jnp.einsum('bjgh,bjgn,bjghp->bghpn', wj, bc, xc)
        return s, y

    s_fin, ys = lax.scan(step, s0.astype(F32).reshape(b, G, HG, P, N), xs)
    return _from_chunks(ys).reshape(b, T, SSM_HEADS, P), s_fin.reshape(b, SSM_HEADS, P, N)


def _hgrn2_scan(q, k, v, g, s0):
    b, T = q.shape[:2]
    L = math.gcd(T, HG_CHUNK)
    xs = tuple(_to_chunks(t.astype(F32), L) for t in (q, k, v, g))
    causal = jnp.tril(jnp.ones((L, L), dtype=bool))[None, :, :, None, None]

    def step(s, inp):
        qc, kc, vc, gc = inp
        gcs = jnp.cumsum(gc, axis=1)
        decay = jnp.exp(jnp.where(causal, gcs[:, :, None] - gcs[:, None, :], -jnp.inf))
        att = jnp.einsum('bihk,bjhk,bijhk->bhij', qc, kc, decay)
        o = jnp.einsum('bhij,bjhv->bihv', att, vc) + jnp.einsum('bihk,bhkv->bihv', qc * jnp.exp(gcs), s)
        last = gcs[:, -1]
        s = jnp.exp(last)[..., None] * s + jnp.einsum('bjhk,bjhv->bhkv', kc * jnp.exp(last[:, None] - gcs), vc)
        return s, o

    s_fin, outs = lax.scan(step, s0.astype(F32), xs)
    return _from_chunks(outs), s_fin


def _mixer(h, conv_buf, ssm_s, hg_s, lb, l, P):
    b, T, _ = h.shape
    z, xbc, dt, q, f, i, og, ga, gb = _split_cols(h @ P['w_in'][l])
    xbc, new_conv = _causal_conv(xbc, conv_buf, P['conv_w'][l], P['conv_b'][l])
    xbc = jax.nn.silu(xbc)
    xs, bm, cm = jnp.split(xbc, [SSM_INNER, SSM_INNER + SSM_GROUPS * SSM_STATE], axis=-1)
    xs = xs.reshape(b, T, SSM_HEADS, SSM_HEAD_DIM).astype(F32)
    bm = bm.reshape(b, T, SSM_GROUPS, SSM_STATE)
    cm = cm.reshape(b, T, SSM_GROUPS, SSM_STATE)
    dt = jax.nn.softplus(dt.astype(F32) + P['dt_bias'][l].astype(F32))
    a = -jnp.exp(P['a_log'][l].astype(F32)) * dt
    y, new_ssm = _ssd_scan(xs * dt[..., None], a, bm, cm, ssm_s)
    y = y + xs * P['d_skip'][l].astype(F32)[:, None]
    y = y.reshape(b, T, SSM_INNER).astype(h.dtype) * jax.nn.silu(z)
    y = _rmsnorm(y.reshape(b, T, SSM_GROUPS, -1), P['ssm_norm_w'][l].reshape(SSM_GROUPS, -1)).reshape(b, T, SSM_INNER)
    fr = f.astype(F32)
    logf = jnp.logaddexp(jnp.log(lb), jnp.log1p(-lb) + jax.nn.log_sigmoid(fr))
    kk = (1.0 - lb) * jax.nn.sigmoid(-fr)
    hs = (b, T, HG_HEADS, HG_HEAD_DIM)
    o, new_hg = _hgrn2_scan(jax.nn.silu(q).reshape(hs), kk.reshape(hs), i.reshape(hs), logf.reshape(hs), hg_s)
    o = _rmsnorm(o.astype(h.dtype), P['hgrn_norm_w'][l]) * jax.nn.silu(og.reshape(hs))
    m = (jax.nn.sigmoid(ga) * (y @ P['w_branch_ssm'][l])
         + jax.nn.sigmoid(gb) * (o.reshape(b, T, HG_DIM) @ P['w_branch_hgrn'][l]))
    return m @ P['w_out'][l], new_conv, new_ssm.astype(ssm_s.dtype), new_hg.astype(hg_s.dtype)


def _moe(h, l, P):
    b, T, Dm = h.shape
    t = h.reshape(b * T, Dm)
    logits = (t @ P['router_w'][l] + P['router_b'][l]).astype(F32)
    top_val, top_idx = lax.top_k(logits, TOP_K)
    probs = jax.nn.softmax(top_val, axis=-1)
    combine = jnp.einsum('tk,tke->te', probs, jax.nn.one_hot(top_idx, N_EXPERTS, dtype=F32)).astype(t.dtype)
    out = jnp.zeros_like(t)
    for e in range(N_EXPERTS):
        gate, up = jnp.split(t @ P['w_gate_up'][l, e] + P['b_gate_up'][l, e], 2, axis=-1)
        gate = jnp.minimum(gate, SWIGLU_LIMIT)
        up = jnp.clip(up, -SWIGLU_LIMIT, SWIGLU_LIMIT)
        act = gate * jax.nn.sigmoid(SWIGLU_ALPHA * gate) * (up + 1.0)
        out = out + combine[:, e:e + 1] * (act @ P['w_down'][l, e] + P['b_down'][l, e])
    return out.reshape(b, T, Dm)


def _trunk(x, c, conv0, ssm0, hg0, lb_all, P):
    new_conv, new_ssm, new_hg = [], [], []
    cs = jax.nn.silu(c)
    for l in range(DEPTH):
        mod = (cs @ P['ada_w'][l] + P['ada_b'][l])[:, None, :]
        sh1, sc1, g1, sh2, sc2, g2 = jnp.split(mod, 6, axis=-1)
        h = _rmsnorm(x, P['norm_mix_w'][l]) * (1.0 + sc1) + sh1
        mix, cv, ss, hg = _mixer(h, conv0[l], ssm0[l], hg0[l], lb_all[l], l, P)
        x = x + g1 * mix
        h = _rmsnorm(x, P['norm_ffn_w'][l]) * (1.0 + sc2) + sh2
        x = x + g2 * _moe(h, l, P)
        new_conv.append(cv)
        new_ssm.append(ss)
        new_hg.append(hg)
    return _rmsnorm(x, P['final_norm_w']), jnp.stack(new_conv), jnp.stack(new_ssm), jnp.stack(new_hg)


def setup_inputs(seed: int = 0) -> dict:
    key = jax.random.key(seed)
    keys = iter(jax.random.split(key, 40))

    def nrm(shape, scale):
        return jax.random.normal(next(keys), shape, jnp.float32) * scale

    dt0 = jnp.exp(jax.random.uniform(next(keys), (DEPTH, SSM_HEADS), jnp.float32, math.log(1e-3), math.log(1e-1)))
    a_log = jnp.log(jax.random.uniform(next(keys), (DEPTH, SSM_HEADS), jnp.float32, 1.0, 16.0))
    return dict(
        x_prompt=nrm((BATCH, SEQ, D_MODEL), 1.0),
        x_sample=nrm((DEC_BATCH, DEC_SEQ, D_MODEL), 1.0),
        c_prompt=nrm((BATCH, D_MODEL), 1.0),
        c_sample=nrm((DEC_BATCH, D_MODEL), 1.0),
        state_ssm=nrm((DEPTH, DEC_BATCH, SSM_HEADS, SSM_HEAD_DIM, SSM_STATE), 0.5),
        state_conv=nrm((DEPTH, DEC_BATCH, SSM_CONV - 1, SSM_CONV_DIM), 1.0),
        state_hgrn=nrm((DEPTH, DEC_BATCH, HG_HEADS, HG_HEAD_DIM, HG_HEAD_DIM), 0.5),
        norm_mix_w=1.0 + nrm((DEPTH, D_MODEL), 0.1),
        norm_ffn_w=1.0 + nrm((DEPTH, D_MODEL), 0.1),
        ada_w=nrm((DEPTH, D_MODEL, 6 * D_MODEL), 0.5 * D_MODEL ** -0.5),
        ada_b=nrm((DEPTH, 6 * D_MODEL), 0.02),
        w_in=nrm((DEPTH, D_MODEL, IN_DIM), D_MODEL ** -0.5),
        conv_w=nrm((DEPTH, SSM_CONV, SSM_CONV_DIM), SSM_CONV ** -0.5),
        conv_b=nrm((DEPTH, SSM_CONV_DIM), 0.02),
        dt_bias=dt0 + jnp.log(-jnp.expm1(-dt0)),
        a_log=a_log,
        d_skip=1.0 + nrm((DEPTH, SSM_HEADS), 0.1),
        ssm_norm_w=1.0 + nrm((DEPTH, SSM_INNER), 0.1),
        hgrn_lb_logits=nrm((DEPTH, HG_DIM), 0.5),
        hgrn_norm_w=1.0 + nrm((DEPTH, HG_HEAD_DIM), 0.1),
        w_branch_ssm=nrm((DEPTH, SSM_INNER, D_MODEL), SSM_INNER ** -0.5),
        w_branch_hgrn=nrm((DEPTH, HG_DIM, D_MODEL), HG_DIM ** -0.5),
        w_out=nrm((DEPTH, D_MODEL, D_MODEL), D_MODEL ** -0.5),
        router_w=nrm((DEPTH, D_MODEL, N_EXPERTS), D_MODEL ** -0.5),
        router_b=nrm((DEPTH, N_EXPERTS), 0.01),
        w_gate_up=nrm((DEPTH, N_EXPERTS, D_MODEL, 2 * D_FF), D_MODEL ** -0.5),
        b_gate_up=nrm((DEPTH, N_EXPERTS, 2 * D_FF), 0.02),
        w_down=nrm((DEPTH, N_EXPERTS, D_FF, D_MODEL), D_FF ** -0.5),
        b_down=nrm((DEPTH, N_EXPERTS, D_MODEL), 0.02),
        final_norm_w=1.0 + nrm((D_MODEL,), 0.1),
    )


def reference(x_prompt, x_sample, c_prompt, c_sample, state_ssm, state_conv, state_hgrn,
              norm_mix_w, norm_ffn_w, ada_w, ada_b, w_in, conv_w, conv_b, dt_bias, a_log, d_skip,
              ssm_norm_w, hgrn_lb_logits, hgrn_norm_w, w_branch_ssm, w_branch_hgrn, w_out,
              router_w, router_b, w_gate_up, b_gate_up, w_down, b_down, final_norm_w):
    P = dict(norm_mix_w=norm_mix_w, norm_ffn_w=norm_ffn_w, ada_w=ada_w, ada_b=ada_b, w_in=w_in,
             conv_w=conv_w, conv_b=conv_b, dt_bias=dt_bias, a_log=a_log, d_skip=d_skip,
             ssm_norm_w=ssm_norm_w, hgrn_norm_w=hgrn_norm_w, w_branch_ssm=w_branch_ssm,
             w_branch_hgrn=w_branch_hgrn, w_out=w_out, router_w=router_w, router_b=router_b,
             w_gate_up=w_gate_up, b_gate_up=b_gate_up, w_down=w_down, b_down=b_down,
             final_norm_w=final_norm_w)
    lb_all = jnp.cumsum(jax.nn.softmax(hgrn_lb_logits.astype(F32), axis=0), axis=0)
    lb_all = lb_all - lb_all[:1]
    bp = x_prompt.shape[0]
    conv0 = jnp.zeros((DEPTH, bp, SSM_CONV - 1, SSM_CONV_DIM), x_prompt.dtype)
    ssm0 = jnp.zeros((DEPTH, bp, SSM_HEADS, SSM_HEAD_DIM, SSM_STATE), x_prompt.dtype)
    hg0 = jnp.zeros((DEPTH, bp, HG_HEADS, HG_HEAD_DIM, HG_HEAD_DIM), x_prompt.dtype)
    y_prompt, conv_p, ssm_p, hg_p = _trunk(x_prompt, c_prompt, conv0, ssm0, hg0, lb_all, P)
    y_sample, conv_s, ssm_s, hg_s = _trunk(x_sample, c_sample, state_conv, state_ssm, state_hgrn, lb_all, P)
    return (y_prompt, y_sample, ssm_p, conv_p, hg_p, ssm_s, conv_s, hg_s)
```

```python
import functools
import math

import jax
import jax.numpy as jnp
from jax import lax
from jax.experimental import pallas as pl
from jax.experimental.pallas import tpu as pltpu

F32 = jnp.float32
BF16 = jnp.bfloat16
I32 = jnp.int32
HI = lax.Precision.HIGHEST

D_MODEL = 1024
SSM_INNER = 2048
SSM_HEAD_DIM = 64
SSM_HEADS = 32
SSM_GROUPS = 4
SSM_GROUP_WIDTH = SSM_INNER // SSM_GROUPS
SSM_STATE = 128
SSM_CONV = 4
SSM_CONV_DIM = 3072
SSM_CHUNK = 128
HG_DIM = 1024
HG_HEAD_DIM = 128
HG_HEADS = 8
HG_CHUNK = 64
HG_BAND = 8
N_EXPERTS = 32
TOP_K = 4
D_FF = 1024
SWIGLU_LIMIT = 7.0
SWIGLU_ALPHA = 1.702
EPS = 1e-6

ROW_TILE = 256
MOD_ROWS = 8
EXPERT_TILE = 256
VMEM_LIMIT_V7X = 56 * 1024 * 1024

_NT = (((1,), (1,)), ((), ()))
_TN = (((0,), (0,)), ((), ()))


def _params(*sem):
    return pltpu.CompilerParams(dimension_semantics=sem, vmem_limit_bytes=VMEM_LIMIT_V7X)


def _dot(a, b):
    return jnp.dot(a, b, preferred_element_type=F32)


def _dot_hi(a, b):
    return jnp.dot(a, b, precision=HI, preferred_element_type=F32)


def _dot_nt(a, b, precision=None):
    return lax.dot_general(a, b, _NT, precision=precision, preferred_element_type=F32)


def _dot_tn(a, b):
    return lax.dot_general(a, b, _TN, preferred_element_type=F32)


def _sigmoid(x):
    return 1.0 / (1.0 + jnp.exp(-x))


def _silu(x):
    return x * _sigmoid(x)


def _softplus(x):
    return jnp.maximum(x, 0.0) + jnp.log1p(jnp.exp(-jnp.abs(x)))


def _full(shape):
    nd = len(shape)
    return pl.BlockSpec(shape, lambda *_: (0,) * nd)


def _ada_kernel(c_ref, w_ref, b_ref, o_ref):
    o_ref[...] = _dot_hi(_silu(c_ref[...]), w_ref[...]) + b_ref[...]


def _ada(c_all, w, b):
    n = c_all.shape[0]
    tn = 1536
    return pl.pallas_call(
        _ada_kernel,
        grid=(6 * D_MODEL // tn,),
        in_specs=[_full((n, D_MODEL)),
                  pl.BlockSpec((D_MODEL, tn), lambda j: (0, j)),
                  pl.BlockSpec((1, tn), lambda j: (0, j))],
        out_specs=pl.BlockSpec((n, tn), lambda j: (0, j)),
        out_shape=jax.ShapeDtypeStruct((n, 6 * D_MODEL), F32),
        compiler_params=_params("arbitrary"),
        name="ada",
    )(c_all, w, b.reshape(1, -1))


def _mod_norm(x, nw, sc, sh):
    ms = jnp.mean(x * x, axis=-1, keepdims=True)
    y = x * lax.rsqrt(ms + EPS) * nw
    y3 = y.reshape(ROW_TILE // MOD_ROWS, MOD_ROWS, D_MODEL)
    return (y3 * (1.0 + sc) + sh).reshape(ROW_TILE, D_MODEL)


def _mod_spec():
    return pl.BlockSpec((ROW_TILE // MOD_ROWS, 1, D_MODEL), lambda i: (i, 0, 0))


def _row_spec(cols):
    return pl.BlockSpec((ROW_TILE, cols), lambda i: (i, 0))


def _inproj_ssd_kernel(x_ref, sc_ref, sh_ref, nw_ref, wz_ref, wx_ref, wdt_ref, dtb_ref, alog_ref,
                       zs_ref, xbc_ref, dt_ref, a_ref):
    h = _mod_norm(x_ref[...], nw_ref[...], sc_ref[...], sh_ref[...])
    hb = h.astype(BF16)
    zs_ref[...] = _silu(_dot(hb, wz_ref[...])).astype(BF16)
    xbc_ref[...] = _dot(hb, wx_ref[...])
    dt = _softplus(_dot_hi(h, wdt_ref[...]) + dtb_ref[...])
    dt_ref[...] = dt
    a_ref[...] = -jnp.exp(alog_ref[...]) * dt


def _inproj_ssd(x, sc, sh, nw, wz, wx, wdt, dtb, alog):
    n = x.shape[0]
    return pl.pallas_call(
        _inproj_ssd_kernel,
        grid=(n // ROW_TILE,),
        in_specs=[_row_spec(D_MODEL), _mod_spec(), _mod_spec(), _full((1, D_MODEL)),
                  _full(wz.shape), _full(wx.shape), _full(wdt.shape), _full((1, SSM_HEADS)), _full((1, SSM_HEADS))],
        out_specs=[_row_spec(SSM_INNER), _row_spec(SSM_CONV_DIM), _row_spec(SSM_HEADS), _row_spec(SSM_HEADS)],
        out_shape=[jax.ShapeDtypeStruct((n, SSM_INNER), BF16), jax.ShapeDtypeStruct((n, SSM_CONV_DIM), F32),
                   jax.ShapeDtypeStruct((n, SSM_HEADS), F32), jax.ShapeDtypeStruct((n, SSM_HEADS), F32)],
        compiler_params=_params("arbitrary"),
        name="inproj_ssd",
    )(x, sc, sh, nw, wz, wx, wdt, dtb, alog)


def _inproj_hgrn_kernel(x_ref, sc_ref, sh_ref, nw_ref, w_ref, loglb_ref, log1mlb_ref, oneml_ref,
                        qs_ref, kk_ref, logf_ref, v_ref, ogs_ref, sga_ref, sgb_ref):
    h = _mod_norm(x_ref[...], nw_ref[...], sc_ref[...], sh_ref[...])
    hb = h.astype(BF16)

    def proj(k):
        return _dot(hb, w_ref[:, k * D_MODEL:(k + 1) * D_MODEL])

    qs_ref[...] = _silu(proj(0)).astype(BF16)
    fr = proj(1)
    log_sig = jnp.minimum(fr, 0.0) - jnp.log1p(jnp.exp(-jnp.abs(fr)))
    a = loglb_ref[...]
    b = log1mlb_ref[...] + log_sig
    logf_ref[...] = jnp.maximum(a, b) + jnp.log1p(jnp.exp(-jnp.abs(a - b)))
    kk_ref[...] = (oneml_ref[...] * _sigmoid(-fr)).astype(BF16)
    v_ref[...] = proj(2).astype(BF16)
    ogs_ref[...] = _silu(proj(3)).astype(BF16)
    sga_ref[...] = _sigmoid(proj(4)).astype(BF16)
    sgb_ref[...] = _sigmoid(proj(5)).astype(BF16)


def _inproj_hgrn(x, sc, sh, nw, w, loglb, log1mlb, oneml):
    n = x.shape[0]
    vec = _full((1, HG_DIM))
    out = lambda dt: jax.ShapeDtypeStruct((n, HG_DIM), dt)
    return pl.pallas_call(
        _inproj_hgrn_kernel,
        grid=(n // ROW_TILE,),
        in_specs=[_row_spec(D_MODEL), _mod_spec(), _mod_spec(), vec, _full(w.shape), vec, vec, vec],
        out_specs=[_row_spec(HG_DIM)] * 7,
        out_shape=[out(BF16), out(BF16), out(F32), out(BF16), out(BF16), out(BF16), out(BF16)],
        compiler_params=_params("arbitrary"),
        name="inproj_hgrn",
    )(x, sc, sh, nw, w, loglb, log1mlb, oneml)


def _ssd_kernel(zs_ref, xbc_ref, dt_ref, a_ref, cw_ref, cb_ref, dsk_ref, nw_ref, tril_ref, e_ref, et_ref, eye_ref,
                *rest, L, has_init):
    if has_init:
        cs0_ref, s0_ref = rest[:2]
    yn_ref, cso_ref, so_ref, s_scr, xp_scr = rest[-5:]
    c = pl.program_id(1)

    @pl.when(c == 0)
    def _():
        xp_scr[0:8, :] = jnp.zeros((8, SSM_CONV_DIM), F32)
        if has_init:
            s_scr[...] = s0_ref[0, 0]
            xp_scr[5:8, :] = cs0_ref[0, 0]
        else:
            s_scr[...] = jnp.zeros(s_scr.shape, F32)

    u = xbc_ref[...]
    xp_scr[8:8 + L, :] = u
    conv = cb_ref[...]
    for k in range(SSM_CONV):
        conv = conv + cw_ref[k:k + 1, :] * xp_scr[5 + k:5 + k + L, :]
    tail = u[L - (SSM_CONV - 1):L, :]
    xp_scr[5:8, :] = tail
    cso_ref[0, 0] = tail
    xbc = _silu(conv)
    xs = xbc[:, :SSM_INNER]

    a = a_ref[...]
    acs = _dot_hi(tril_ref[...], a)
    acs_t = _dot_nt(eye_ref[...], acs, precision=HI)
    last = acs[L - 1:L, :]
    expand = e_ref[...]
    dt_e = _dot_hi(dt_ref[...], expand)
    eacs_e = _dot_hi(jnp.exp(acs), expand)
    wj_e = _dot_hi(jnp.exp(last - acs), expand)
    xdt = xs * dt_e
    xdt_b = xdt.astype(BF16)
    xw_b = (xdt * wj_e).astype(BF16)
    row_scale = _dot_nt(et_ref[...], jnp.broadcast_to(jnp.exp(last), (SSM_STATE, SSM_HEADS)), precision=HI)

    ii = lax.broadcasted_iota(I32, (L, L), 0)
    jj = lax.broadcasted_iota(I32, (L, L), 1)
    causal = ii >= jj
    first_head = lax.broadcasted_iota(I32, (L, 2 * SSM_HEAD_DIM), 1) < SSM_HEAD_DIM

    y_intra, y_state = [], []
    for g in range(SSM_GROUPS):
        b_g = xbc[:, SSM_INNER + g * SSM_STATE:SSM_INNER + (g + 1) * SSM_STATE].astype(BF16)
        c_off = SSM_INNER + SSM_GROUPS * SSM_STATE
        c_g = xbc[:, c_off + g * SSM_STATE:c_off + (g + 1) * SSM_STATE].astype(BF16)
        scores = _dot_nt(c_g, b_g)
        rows = slice(g * SSM_GROUP_WIDTH, (g + 1) * SSM_GROUP_WIDTH)
        s_g = s_scr[rows, :]
        y_state.append(_dot_nt(c_g, s_g.astype(BF16)))
        for pair in range(SSM_GROUP_WIDTH // (2 * SSM_HEAD_DIM)):
            h0 = g * (SSM_HEADS // SSM_GROUPS) + 2 * pair
            ms = []
            for h in (h0, h0 + 1):
                decay = jnp.where(causal, jnp.exp(acs[:, h:h + 1] - acs_t[h:h + 1, :]), 0.0)
                ms.append(scores * decay)
            m_pair = jnp.concatenate(ms, axis=1).astype(BF16)
            x_pair = xdt_b[:, h0 * SSM_HEAD_DIM:(h0 + 2) * SSM_HEAD_DIM]
            zero = jnp.zeros_like(x_pair)
            x_bd = jnp.concatenate([jnp.where(first_head, x_pair, zero), jnp.where(first_head, zero, x_pair)], axis=0)
            y_intra.append(_dot(m_pair, x_bd))
        s_scr[rows, :] = s_g * row_scale[rows, :] + _dot_tn(xw_b[:, rows], b_g)

    y = jnp.concatenate(y_intra, axis=1) + jnp.concatenate(y_state, axis=1) * eacs_e + xs * dsk_ref[...]
    yz = y * zs_ref[...].astype(F32)
    normed = []
    for g in range(SSM_GROUPS):
        t = yz[:, g * SSM_GROUP_WIDTH:(g + 1) * SSM_GROUP_WIDTH]
        normed.append(t * lax.rsqrt(jnp.mean(t * t, axis=-1, keepdims=True) + EPS))
    yn_ref[...] = (jnp.concatenate(normed, axis=1) * nw_ref[...]).astype(BF16)

    @pl.when(c == pl.num_programs(1) - 1)
    def _():
        so_ref[0, 0] = s_scr[...]


def _alias_previous(prev, args, in_specs):
    aliases = {}
    for k, p in enumerate(prev):
        if p is not None:
            aliases[len(args)] = k
            args.append(p)
            in_specs.append(pl.BlockSpec(memory_space=pl.ANY))
    return aliases


def _ssd(zs, xbc, dt, a, cw, cb, dsk_e, nw, *, row0, nb, T, L, depth, layer, init, prev):
    n = zs.shape[0]
    nc = T // L
    blk0 = row0 // L
    row = lambda cols: pl.BlockSpec((L, cols), lambda b, c: (blk0 + b * nc + c, 0))
    tril = jnp.tril(jnp.ones((L, L), F32))
    expand = jnp.repeat(jnp.eye(SSM_HEADS, dtype=F32), SSM_HEAD_DIM, axis=1)
    eye = jnp.eye(SSM_HEADS, dtype=F32)
    args = [zs, xbc, dt, a, cw, cb, dsk_e, nw, tril, expand, expand.T, eye]
    in_specs = [row(SSM_INNER), row(SSM_CONV_DIM), row(SSM_HEADS), row(SSM_HEADS),
                _full(cw.shape), _full(cb.shape), _full(dsk_e.shape), _full(nw.shape),
                _full(tril.shape), _full(expand.shape), _full((SSM_INNER, SSM_HEADS)), _full(eye.shape)]
    if init is not None:
        args += [init[0], init[1]]
        in_specs += [pl.BlockSpec((1, 1, SSM_CONV - 1, SSM_CONV_DIM), lambda b, c: (layer, b, 0, 0)),
                     pl.BlockSpec((1, 1, SSM_INNER, SSM_STATE), lambda b, c: (layer, b, 0, 0))]
    out_shape = [jax.ShapeDtypeStruct((n, SSM_INNER), BF16),
                 jax.ShapeDtypeStruct((depth, nb, SSM_CONV - 1, SSM_CONV_DIM), F32),
                 jax.ShapeDtypeStruct((depth, nb, SSM_INNER, SSM_STATE), F32)]
    aliases = _alias_previous(prev, args, in_specs)
    return pl.pallas_call(
        functools.partial(_ssd_kernel, L=L, has_init=init is not None),
        grid=(nb, nc),
        in_specs=in_specs,
        out_specs=[row(SSM_INNER),
                   pl.BlockSpec((1, 1, SSM_CONV - 1, SSM_CONV_DIM), lambda b, c: (layer, b, 0, 0)),
                   pl.BlockSpec((1, 1, SSM_INNER, SSM_STATE), lambda b, c: (layer, b, 0, 0))],
        out_shape=out_shape,
        scratch_shapes=[pltpu.VMEM((SSM_INNER, SSM_STATE), F32), pltpu.VMEM((L + 8, SSM_CONV_DIM), F32)],
        input_output_aliases=aliases,
        compiler_params=_params("arbitrary", "arbitrary"),
        name=f"ssd_L{L}",
    )(*args)


def _hgrn_levels(C):
    s, out = C // 2, []
    while s >= HG_BAND:
        out.append(s)
        s //= 2
    return out


def _hgrn_kernel(q_ref, k_ref, v_ref, g_ref, og_ref, nw_ref, tril_ref, ones_ref, *rest, C, has_init):
    if has_init:
        s0_ref = rest[0]
    o_ref, so_ref, st_scr, kp, gp, vp, pbuf, rbuf = rest[-8:]
    c = pl.program_id(1)
    K = HG_HEAD_DIM

    @pl.when(c == 0)
    def _():
        zpad = jnp.zeros((8, HG_DIM), F32)
        kp[0:8, :] = zpad
        gp[0:8, :] = zpad
        vp[0:8, :] = zpad
        for h in range(HG_HEADS):
            st_scr[h] = s0_ref[0, 0, h].T if has_init else jnp.zeros((K, K), F32)

    q = q_ref[...].astype(F32)
    k = k_ref[...].astype(F32)
    v = v_ref[...].astype(F32)
    gcs = _dot_hi(tril_ref[...], g_ref[...])
    kp[8:8 + C, :] = k
    gp[8:8 + C, :] = gcs
    vp[8:8 + C, :] = v

    row_in_block = lax.broadcasted_iota(I32, (C, HG_DIM), 0) % HG_BAND
    for d in range(HG_BAND):
        ks = kp[8 - d:8 - d + C, :]
        gs = gp[8 - d:8 - d + C, :]
        e = jnp.where(row_in_block >= d, jnp.exp(gcs - gs), 0.0)
        p = (q * ks * e).astype(BF16)
        for h in range(HG_HEADS):
            r0 = (d * HG_HEADS + h) * C
            pbuf[r0:r0 + C, :] = p[:, h * K:(h + 1) * K]
    rbuf[...] = _dot(pbuf[...], ones_ref[...])

    ii = lax.broadcasted_iota(I32, (C, C), 0)
    jj = lax.broadcasted_iota(I32, (C, C), 1)
    levels = _hgrn_levels(C)
    att = [jnp.zeros((C, C), F32) for _ in range(HG_HEADS)]
    for s in levels:
        q_parts, k_parts = [], []
        zero = jnp.zeros((s, HG_DIM), F32)
        for lo in range(0, C, 2 * s):
            mid = lo + s
            ref_row = gcs[mid - 1:mid, :]
            q_parts += [zero, q[mid:mid + s] * jnp.exp(gcs[mid:mid + s] - ref_row)]
            k_parts += [k[lo:mid] * jnp.exp(ref_row - gcs[lo:mid]), zero]
        qt = jnp.concatenate(q_parts, axis=0).astype(BF16)
        kt = jnp.concatenate(k_parts, axis=0).astype(BF16)
        same_segment = (ii // (2 * s)) == (jj // (2 * s))
        for h in range(HG_HEADS):
            att[h] = att[h] + jnp.where(same_segment, _dot_nt(qt[:, h * K:(h + 1) * K], kt[:, h * K:(h + 1) * K]), 0.0)

    last = gcs[C - 1:C, :]
    q_state = (q * jnp.exp(gcs)).astype(BF16)
    k_last = (k * jnp.exp(last - gcs)).astype(BF16)
    e_last = jnp.exp(last)
    v_b = v_ref[...]
    outs = []
    for h in range(HG_HEADS):
        cols = slice(h * K, (h + 1) * K)
        st = st_scr[h]
        o = _dot_nt(q_state[:, cols], st.astype(BF16))
        if levels:
            o = o + _dot(att[h].astype(BF16), v_b[:, cols])
        for d in range(HG_BAND):
            r0 = (d * HG_HEADS + h) * C
            o = o + rbuf[r0:r0 + C, :] * vp[8 - d:8 - d + C, cols]
        st_scr[h] = st * e_last[:, cols] + _dot_tn(v_b[:, cols], k_last[:, cols])
        on = o * lax.rsqrt(jnp.mean(o * o, axis=-1, keepdims=True) + EPS)
        outs.append(on)
    o_all = jnp.concatenate(outs, axis=1) * nw_ref[...] * og_ref[...].astype(F32)
    o_ref[...] = o_all.astype(BF16)

    @pl.when(c == pl.num_programs(1) - 1)
    def _():
        for h in range(HG_HEADS):
            so_ref[0, 0, h] = st_scr[h].T


def _hgrn(qs, kk, v, logf, ogs, nw_e, *, row0, nb, T, C, depth, layer, init, prev):
    n = qs.shape[0]
    nc = T // C
    blk0 = row0 // C
    row = pl.BlockSpec((C, HG_DIM), lambda b, c: (blk0 + b * nc + c, 0))
    K = HG_HEAD_DIM
    tril = jnp.tril(jnp.ones((C, C), F32))
    ones = jnp.ones((K, K), BF16)
    args = [qs, kk, v, logf, ogs, nw_e, tril, ones]
    in_specs = [row, row, row, row, row, _full(nw_e.shape), _full(tril.shape), _full(ones.shape)]
    state_spec = pl.BlockSpec((1, 1, HG_HEADS, K, K), lambda b, c: (layer, b, 0, 0, 0))
    if init is not None:
        args.append(init)
        in_specs.append(state_spec)
    out_shape = [jax.ShapeDtypeStruct((n, HG_DIM), BF16),
                 jax.ShapeDtypeStruct((depth, nb, HG_HEADS, K, K), F32)]
    aliases = _alias_previous(prev, args, in_specs)
    band_rows = HG_BAND * HG_HEADS * C
    return pl.pallas_call(
        functools.partial(_hgrn_kernel, C=C, has_init=init is not None),
        grid=(nb, nc),
        in_specs=in_specs,
        out_specs=[row, state_spec],
        out_shape=out_shape,
        scratch_shapes=[pltpu.VMEM((HG_HEADS, K, K), F32),
                        pltpu.VMEM((C + 8, HG_DIM), F32), pltpu.VMEM((C + 8, HG_DIM), F32),
                        pltpu.VMEM((C + 8, HG_DIM), F32),
                        pltpu.VMEM((band_rows, K), BF16), pltpu.VMEM((band_rows, K), F32)],
        input_output_aliases=aliases,
        compiler_params=_params("arbitrary", "arbitrary"),
        name=f"hgrn_C{C}",
    )(*args)


def _merge_kernel(yn_ref, o_ref, sga_ref, sgb_ref, x_ref, g1_ref, sc2_ref, sh2_ref, nw2_ref,
                  wbs_ref, wbh_ref, wout_ref, rw_ref, rb_ref, xo_ref, h2_ref, idx_ref, p_ref):
    m = (sga_ref[...].astype(F32) * _dot(yn_ref[...], wbs_ref[...])
         + sgb_ref[...].astype(F32) * _dot(o_ref[...], wbh_ref[...]))
    mo = _dot(m.astype(BF16), wout_ref[...])
    x3 = x_ref[...].reshape(ROW_TILE // MOD_ROWS, MOD_ROWS, D_MODEL)
    mo3 = mo.reshape(ROW_TILE // MOD_ROWS, MOD_ROWS, D_MODEL)
    xn = (x3 + g1_ref[...] * mo3).reshape(ROW_TILE, D_MODEL)
    xo_ref[...] = xn
    h2 = _mod_norm(xn, nw2_ref[...], sc2_ref[...], sh2_ref[...])
    h2_ref[...] = h2
    logits = _dot_hi(h2, rw_ref[...].astype(BF16).astype(F32)) + rb_ref[...]
    lane = lax.broadcasted_iota(I32, logits.shape, 1)
    vals, idxs = [], []
    for _ in range(TOP_K):
        mx = jnp.max(logits, axis=-1, keepdims=True)
        first = jnp.min(jnp.where(logits == mx, lane, N_EXPERTS), axis=-1, keepdims=True)
        vals.append(mx)
        idxs.append(first)
        logits = jnp.where(lane == first, -jnp.inf, logits)
    tv = jnp.concatenate(vals, axis=1)
    ev = jnp.exp(tv - vals[0])
    p_ref[...] = ev / jnp.sum(ev, axis=-1, keepdims=True)
    idx_ref[...] = jnp.concatenate(idxs, axis=1)


def _merge(yn, o, sga, sgb, x, g1, sc2, sh2, nw2, wbs, wbh, wout, rw, rb):
    n = x.shape[0]
    return pl.pallas_call(
        _merge_kernel,
        grid=(n // ROW_TILE,),
        in_specs=[_row_spec(SSM_INNER), _row_spec(HG_DIM), _row_spec(D_MODEL), _row_spec(D_MODEL), _row_spec(D_MODEL),
                  _mod_spec(), _mod_spec(), _mod_spec(), _full((1, D_MODEL)),
                  _full(wbs.shape), _full(wbh.shape), _full(wout.shape), _full(rw.shape), _full((1, N_EXPERTS))],
        out_specs=[_row_spec(D_MODEL), _row_spec(D_MODEL), _row_spec(TOP_K), _row_spec(TOP_K)],
        out_shape=[jax.ShapeDtypeStruct((n, D_MODEL), F32), jax.ShapeDtypeStruct((n, D_MODEL), F32),
                   jax.ShapeDtypeStruct((n, TOP_K), I32), jax.ShapeDtypeStruct((n, TOP_K), F32)],
        compiler_params=_params("arbitrary"),
        name="merge_router",
    )(yn, o, sga, sgb, x, g1, sc2, sh2, nw2, wbs, wbh, wout, rw, rb)


def _gather_copy(src_hbm, row, dst, dst_row, sem):
    return pltpu.make_async_copy(src_hbm.at[pl.ds(row, 1), :], dst.at[pl.ds(dst_row, 1), :], sem)


def _start_gather(idx_ref, n_rows, src_hbm, dst, sem):
    def body(r, carry):
        _gather_copy(src_hbm, idx_ref[0, 0, r], dst, r, sem).start()
        return carry
    lax.fori_loop(0, n_rows, body, 0, unroll=8)


def _wait_gather(n_rows, src_hbm, dst, sem):
    pltpu.make_async_copy(src_hbm.at[pl.ds(0, n_rows), :], dst, sem).wait()


def _gmm_kernel(te_ref, nused_ref, idx_ref, idx_next_ref, h_hbm, wgu_ref, bgu_ref, wd_ref, bd_ref,
                y_ref, xbuf, sems, wgu_b, wd_b):
    i = pl.program_id(0)
    n_used = nused_ref[0]
    slot = i % 2

    @pl.when(i == 0)
    def _():
        _start_gather(idx_ref, EXPERT_TILE, h_hbm, xbuf.at[0], sems.at[0])

    @pl.when(i + 1 < n_used)
    def _():
        _start_gather(idx_next_ref, EXPERT_TILE, h_hbm, xbuf.at[1 - slot], sems.at[1 - slot])

    new_expert = jnp.logical_or(i == 0, te_ref[i] != te_ref[jnp.maximum(i - 1, 0)])

    @pl.when(jnp.logical_and(i < n_used, new_expert))
    def _():
        wgu_b[...] = wgu_ref[0].astype(BF16)
        wd_b[...] = wd_ref[0].astype(BF16)

    @pl.when(i < n_used)
    def _():
        _wait_gather(EXPERT_TILE, h_hbm, xbuf.at[slot], sems.at[slot])
        x = xbuf[slot].astype(BF16)
        gu = _dot(x, wgu_b[...]) + bgu_ref[0]
        gate = jnp.minimum(gu[:, :D_FF], SWIGLU_LIMIT)
        up = jnp.clip(gu[:, D_FF:], -SWIGLU_LIMIT, SWIGLU_LIMIT)
        act = gate * _sigmoid(SWIGLU_ALPHA * gate) * (up + 1.0)
        y_ref[...] = _dot(act.astype(BF16), wd_b[...]) + bd_ref[0]

    @pl.when(i >= n_used)
    def _():
        y_ref[...] = jnp.zeros(y_ref.shape, F32)


def _gmm(tile_expert, n_used, src_rows, h2, wgu, bgu, wd, bd, layer):
    n_tiles = tile_expert.shape[0]
    idx3 = src_rows.reshape(n_tiles, 1, EXPERT_TILE)
    smem_idx = lambda f: pl.BlockSpec((1, 1, EXPERT_TILE), f, memory_space=pltpu.SMEM)
    grid_spec = pltpu.PrefetchScalarGridSpec(
        num_scalar_prefetch=2,
        grid=(n_tiles,),
        in_specs=[smem_idx(lambda i, te, nu: (i, 0, 0)),
                  smem_idx(lambda i, te, nu: (jnp.minimum(i + 1, n_tiles - 1), 0, 0)),
                  pl.BlockSpec(memory_space=pl.ANY),
                  pl.BlockSpec((1, 1, D_MODEL, 2 * D_FF), lambda i, te, nu: (layer, te[i], 0, 0)),
                  pl.BlockSpec((1, 1, 1, 2 * D_FF), lambda i, te, nu: (layer, te[i], 0, 0)),
                  pl.BlockSpec((1, 1, D_FF, D_MODEL), lambda i, te, nu: (layer, te[i], 0, 0)),
                  pl.BlockSpec((1, 1, 1, D_MODEL), lambda i, te, nu: (layer, te[i], 0, 0))],
        out_specs=pl.BlockSpec((EXPERT_TILE, D_MODEL), lambda i, te, nu: (i, 0)),
        scratch_shapes=[pltpu.VMEM((2, EXPERT_TILE, D_MODEL), F32), pltpu.SemaphoreType.DMA((2,)),
                        pltpu.VMEM((D_MODEL, 2 * D_FF), BF16), pltpu.VMEM((D_FF, D_MODEL), BF16)],
    )
    return pl.pallas_call(
        _gmm_kernel_4d,
        grid_spec=grid_spec,
        out_shape=jax.ShapeDtypeStruct((n_tiles * EXPERT_TILE, D_MODEL), F32),
        compiler_params=_params("arbitrary"),
        name="moe_gmm",
    )(tile_expert, n_used, idx3, idx3, h2, wgu, bgu.reshape(bgu.shape[0], bgu.shape[1], 1, -1), wd,
      bd.reshape(bd.shape[0], bd.shape[1], 1, -1))


def _gmm_kernel_4d(te_ref, nused_ref, idx_ref, idx_next_ref, h_hbm, wgu_ref, bgu_ref, wd_ref, bd_ref, *rest):
    return _gmm_kernel(te_ref, nused_ref, idx_ref, idx_next_ref, h_hbm, wgu_ref.at[0], bgu_ref.at[0],
                       wd_ref.at[0], bd_ref.at[0], *rest)


def _combine_kernel(slot_ref, slot_next_ref, p_ref, x_ref, g2_ref, fw_ref, y_hbm, xo_ref, gbuf, sems, *, final):
    i = pl.program_id(0)
    n = pl.num_programs(0)
    cur = i % 2
    rows = TOP_K * ROW_TILE

    @pl.when(i == 0)
    def _():
        _start_gather(slot_ref, rows, y_hbm, gbuf.at[0], sems.at[0])

    @pl.when(i + 1 < n)
    def _():
        _start_gather(slot_next_ref, rows, y_hbm, gbuf.at[1 - cur], sems.at[1 - cur])

    _wait_gather(rows, y_hbm, gbuf.at[cur], sems.at[cur])
    p = p_ref[...]
    acc = jnp.zeros((ROW_TILE, D_MODEL), F32)
    for k in range(TOP_K):
        acc = acc + p[:, k:k + 1] * gbuf[cur, k * ROW_TILE:(k + 1) * ROW_TILE, :]
    x3 = x_ref[...].reshape(ROW_TILE // MOD_ROWS, MOD_ROWS, D_MODEL)
    acc3 = acc.reshape(ROW_TILE // MOD_ROWS, MOD_ROWS, D_MODEL)
    xn = (x3 + g2_ref[...] * acc3).reshape(ROW_TILE, D_MODEL)
    if final:
        xn = xn * lax.rsqrt(jnp.mean(xn * xn, axis=-1, keepdims=True) + EPS) * fw_ref[...]
    xo_ref[...] = xn


def _combine(slots, p, x, g2, fw, y_sorted, final):
    n = x.shape[0]
    n_tiles = n // ROW_TILE
    rows = TOP_K * ROW_TILE
    slot3 = slots.reshape(n_tiles, ROW_TILE, TOP_K).transpose(0, 2, 1).reshape(n_tiles, 1, rows)
    smem_idx = lambda f: pl.BlockSpec((1, 1, rows), f, memory_space=pltpu.SMEM)
    return pl.pallas_call(
        functools.partial(_combine_kernel, final=final),
        grid=(n_tiles,),
        in_specs=[smem_idx(lambda i: (i, 0, 0)), smem_idx(lambda i: (jnp.minimum(i + 1, n_tiles - 1), 0, 0)),
                  _row_spec(TOP_K), _row_spec(D_MODEL), _mod_spec(), _full((1, D_MODEL)),
                  pl.BlockSpec(memory_space=pl.ANY)],
        out_specs=_row_spec(D_MODEL),
        out_shape=jax.ShapeDtypeStruct((n, D_MODEL), F32),
        scratch_shapes=[pltpu.VMEM((2, rows, D_MODEL), F32), pltpu.SemaphoreType.DMA((2,))],
        compiler_params=_params("arbitrary"),
        name="moe_combine",
    )(slot3, slot3, p, x, g2, fw, y_sorted)


def _route(idx):
    n = idx.shape[0]
    e = idx.reshape(-1)
    onehot = (e[:, None] == jnp.arange(N_EXPERTS, dtype=I32)[None, :]).astype(I32)
    csum = jnp.cumsum(onehot, axis=0)
    rank = jnp.sum(csum * onehot, axis=1) - 1
    counts = csum[-1]
    padded = (counts + EXPERT_TILE - 1) // EXPERT_TILE * EXPERT_TILE
    ends = jnp.cumsum(padded)
    slots = (ends - padded)[e] + rank
    n_tiles = (n * TOP_K) // EXPERT_TILE + N_EXPERTS
    n_used = (ends[-1] // EXPERT_TILE).astype(I32)
    tile_start = jnp.arange(n_tiles, dtype=I32) * EXPERT_TILE
    te = jnp.minimum(jnp.searchsorted(ends, tile_start, side="right").astype(I32), N_EXPERTS - 1)
    te = jnp.where(tile_start < ends[-1], te, te[jnp.maximum(n_used - 1, 0)])
    token = jnp.arange(n * TOP_K, dtype=I32) // TOP_K
    src = jnp.zeros((n_tiles * EXPERT_TILE,), I32).at[slots].set(token)
    return slots.astype(I32), src, te, n_used.reshape(1)


def kernel(x_prompt, x_sample, c_prompt, c_sample, state_ssm, state_conv, state_hgrn, norm_mix_w, norm_ffn_w, ada_w, ada_b, w_in, conv_w, conv_b, dt_bias, a_log, d_skip, ssm_norm_w, hgrn_lb_logits, hgrn_norm_w, w_branch_ssm, w_branch_hgrn, w_out, router_w, router_b, w_gate_up, b_gate_up, w_down, b_down, final_norm_w):
    nbp, tp, _ = x_prompt.shape
    nbs, ts, _ = x_sample.shape
    depth = w_in.shape[0]
    n_p, n_s = nbp * tp, nbs * ts
    assert depth == 2 and tp % ROW_TILE == 0 and n_s % ROW_TILE == 0 and ts == MOD_ROWS
    l_p, l_s = math.gcd(tp, SSM_CHUNK), math.gcd(ts, SSM_CHUNK)
    c_p, c_s = math.gcd(tp, HG_CHUNK), math.gcd(ts, HG_CHUNK)
    assert c_s == HG_BAND and c_p % HG_BAND == 0

    x = jnp.concatenate([x_prompt.reshape(n_p, D_MODEL), x_sample.reshape(n_s, D_MODEL)], axis=0)
    c_all = jnp.concatenate([c_prompt, c_sample], axis=0)

    lb_all = jnp.cumsum(jax.nn.softmax(hgrn_lb_logits.astype(F32), axis=0), axis=0)
    lb_all = lb_all - lb_all[:1]

    state_ssm2 = state_ssm.reshape(depth, nbs, SSM_INNER, SSM_STATE)
    offs = [0, 2048, 5120, 5152, 6176, 7200, 8224, 9248, 10272, 11296]

    ssd_p = ssd_s = [None, None, None]
    hg_p = hg_s = [None, None]
    for l in range(depth):
        mod = _ada(c_all, ada_w[l], ada_b[l])
        mod_rows = jnp.concatenate([jnp.repeat(mod[:nbp], tp // MOD_ROWS, axis=0), mod[nbp:]], axis=0)
        sh1, sc1, g1, sh2, sc2, g2 = [m.reshape(-1, 1, D_MODEL) for m in jnp.split(mod_rows, 6, axis=-1)]

        wl = w_in[l]
        wz = wl[:, offs[0]:offs[1]].astype(BF16)
        wx = wl[:, offs[1]:offs[2]].astype(BF16)
        wdt = wl[:, offs[2]:offs[3]]
        whg = wl[:, offs[3]:].astype(BF16)
        nw1 = norm_mix_w[l].reshape(1, -1)

        zs, xbc, dt, a = _inproj_ssd(x, sc1, sh1, nw1, wz, wx, wdt, dt_bias[l].reshape(1, -1), a_log[l].reshape(1, -1))
        lb = lb_all[l].reshape(1, -1)
        qs, kk, logf, v, ogs, sga, sgb = _inproj_hgrn(x, sc1, sh1, nw1, whg, jnp.log(lb), jnp.log1p(-lb), 1.0 - lb)

        ssd_args = (zs, xbc, dt, a, conv_w[l], conv_b[l].reshape(1, -1),
                    jnp.repeat(d_skip[l], SSM_HEAD_DIM).reshape(1, -1), ssm_norm_w[l].reshape(1, -1))
        ssd_p = _ssd(*ssd_args, row0=0, nb=nbp, T=tp, L=l_p, depth=depth, layer=l, init=None,
                     prev=[None] + list(ssd_p[1:]))
        ssd_s = _ssd(*ssd_args, row0=n_p, nb=nbs, T=ts, L=l_s, depth=depth, layer=l, init=(state_conv, state_ssm2),
                     prev=[ssd_p[0]] + list(ssd_s[1:]))
        yn = ssd_s[0]

        hg_args = (qs, kk, v, logf, ogs, jnp.tile(hgrn_norm_w[l], HG_HEADS).reshape(1, -1))
        hg_p = _hgrn(*hg_args, row0=0, nb=nbp, T=tp, C=c_p, depth=depth, layer=l, init=None,
                     prev=[None] + list(hg_p[1:]))
        hg_s = _hgrn(*hg_args, row0=n_p, nb=nbs, T=ts, C=c_s, depth=depth, layer=l, init=state_hgrn,
                     prev=[hg_p[0]] + list(hg_s[1:]))
        o = hg_s[0]

        x1, h2, idx, probs = _merge(yn, o, sga, sgb, x, g1, sc2, sh2, norm_ffn_w[l].reshape(1, -1),
                                    w_branch_ssm[l].astype(BF16), w_branch_hgrn[l].astype(BF16),
                                    w_out[l].astype(BF16), router_w[l], router_b[l].reshape(1, -1))
        slots, src, te, n_used = _route(idx)
        y_sorted = _gmm(te, n_used, src, h2, w_gate_up, b_gate_up, w_down, b_down, l)
        x = _combine(slots, probs, x1, g2, final_norm_w.reshape(1, -1), y_sorted, final=(l == depth - 1))

    y_prompt = x[:n_p].reshape(nbp, tp, D_MODEL)
    y_sample = x[n_p:].reshape(nbs, ts, D_MODEL)
    ssm_p = ssd_p[2].reshape(depth, nbp, SSM_HEADS, SSM_HEAD_DIM, SSM_STATE)
    ssm_s = ssd_s[2].reshape(depth, nbs, SSM_HEADS, SSM_HEAD_DIM, SSM_STATE)
    return (y_prompt, y_sample, ssm_p, ssd_p[1], hg_p[1], ssm_s, ssd_s[1], hg_s[1])
```

```python
import functools
import math

import jax
import jax.numpy as jnp
from jax import lax
from jax.experimental import pallas as pl
from jax.experimental.pallas import tpu as pltpu

F32 = jnp.float32
BF16 = jnp.bfloat16
I32 = jnp.int32
HI = lax.Precision.HIGHEST

D_MODEL = 1024
SSM_INNER = 2048
SSM_HEAD_DIM = 64
SSM_HEADS = 32
SSM_GROUPS = 4
SSM_GROUP_WIDTH = SSM_INNER // SSM_GROUPS
SSM_STATE = 128
SSM_CONV = 4
SSM_CONV_DIM = 3072
SSM_CHUNK = 128
HG_DIM = 1024
HG_HEAD_DIM = 128
HG_HEADS = 8
HG_CHUNK = 64
HG_BAND = 8
N_EXPERTS = 32
TOP_K = 4
D_FF = 1024
SWIGLU_LIMIT = 7.0
SWIGLU_ALPHA = 1.702
EPS = 1e-6

ROW_TILE = 256
MOD_ROWS = 8
EXPERT_TILE = 256
VMEM_LIMIT_V7X = 56 * 1024 * 1024

_NT = (((1,), (1,)), ((), ()))
_TN = (((0,), (0,)), ((), ()))


def _params(*sem):
    return pltpu.CompilerParams(dimension_semantics=sem, vmem_limit_bytes=VMEM_LIMIT_V7X)


def _dot(a, b):
    return jnp.dot(a, b, preferred_element_type=F32)


def _dot_hi(a, b):
    return jnp.dot(a, b, precision=HI, preferred_element_type=F32)


def _dot_nt(a, b, precision=None):
    return lax.dot_general(a, b, _NT, precision=precision, preferred_element_type=F32)


def _dot_tn(a, b):
    return lax.dot_general(a, b, _TN, preferred_element_type=F32)


def _sigmoid(x):
    return 1.0 / (1.0 + jnp.exp(-x))


def _silu(x):
    return x * _sigmoid(x)


def _softplus(x):
    return jnp.maximum(x, 0.0) + jnp.log1p(jnp.exp(-jnp.abs(x)))


def _full(shape):
    nd = len(shape)
    return pl.BlockSpec(shape, lambda *_: (0,) * nd)


def _ada_kernel(c_ref, w_ref, b_ref, o_ref):
    o_ref[...] = _dot_hi(_silu(c_ref[...]), w_ref[...]) + b_ref[...]


def _ada(c_all, w, b):
    n = c_all.shape[0]
    tn = 1536
    return pl.pallas_call(
        _ada_kernel,
        grid=(6 * D_MODEL // tn,),
        in_specs=[_full((n, D_MODEL)),
                  pl.BlockSpec((D_MODEL, tn), lambda j: (0, j)),
                  pl.BlockSpec((1, tn), lambda j: (0, j))],
        out_specs=pl.BlockSpec((n, tn), lambda j: (0, j)),
        out_shape=jax.ShapeDtypeStruct((n, 6 * D_MODEL), F32),
        compiler_params=_params("arbitrary"),
        name="ada",
    )(c_all, w, b.reshape(1, -1))


def _mod_norm(x, nw, sc, sh):
    ms = jnp.mean(x * x, axis=-1, keepdims=True)
    y = x * lax.rsqrt(ms + EPS) * nw
    y3 = y.reshape(ROW_TILE // MOD_ROWS, MOD_ROWS, D_MODEL)
    return (y3 * (1.0 + sc) + sh).reshape(ROW_TILE, D_MODEL)


def _mod_spec():
    return pl.BlockSpec((ROW_TILE // MOD_ROWS, 1, D_MODEL), lambda i: (i, 0, 0))


def _row_spec(cols):
    return pl.BlockSpec((ROW_TILE, cols), lambda i: (i, 0))


def _inproj_ssd_kernel(x_ref, sc_ref, sh_ref, nw_ref, wz_ref, wx_ref, wdt_ref, dtb_ref, alog_ref,
                       zs_ref, xbc_ref, dt_ref, a_ref):
    h = _mod_norm(x_ref[...], nw_ref[...], sc_ref[...], sh_ref[...])
    hb = h.astype(BF16)
    zs_ref[...] = _silu(_dot(hb, wz_ref[...])).astype(BF16)
    xbc_ref[...] = _dot(hb, wx_ref[...])
    dt = _softplus(_dot_hi(h, wdt_ref[...]) + dtb_ref[...])
    dt_ref[...] = dt
    a_ref[...] = -jnp.exp(alog_ref[...]) * dt


def _inproj_ssd(x, sc, sh, nw, wz, wx, wdt, dtb, alog):
    n = x.shape[0]
    return pl.pallas_call(
        _inproj_ssd_kernel,
        grid=(n // ROW_TILE,),
        in_specs=[_row_spec(D_MODEL), _mod_spec(), _mod_spec(), _full((1, D_MODEL)),
                  _full(wz.shape), _full(wx.shape), _full(wdt.shape), _full((1, SSM_HEADS)), _full((1, SSM_HEADS))],
        out_specs=[_row_spec(SSM_INNER), _row_spec(SSM_CONV_DIM), _row_spec(SSM_HEADS), _row_spec(SSM_HEADS)],
        out_shape=[jax.ShapeDtypeStruct((n, SSM_INNER), BF16), jax.ShapeDtypeStruct((n, SSM_CONV_DIM), F32),
                   jax.ShapeDtypeStruct((n, SSM_HEADS), F32), jax.ShapeDtypeStruct((n, SSM_HEADS), F32)],
        compiler_params=_params("arbitrary"),
        name="inproj_ssd",
    )(x, sc, sh, nw, wz, wx, wdt, dtb, alog)


def _inproj_hgrn_kernel(x_ref, sc_ref, sh_ref, nw_ref, w_ref, loglb_ref, log1mlb_ref, oneml_ref,
                        qs_ref, kk_ref, logf_ref, v_ref, ogs_ref):
    h = _mod_norm(x_ref[...], nw_ref[...], sc_ref[...], sh_ref[...])
    hb = h.astype(BF16)

    def proj(k):
        return _dot(hb, w_ref[:, k * D_MODEL:(k + 1) * D_MODEL])

    qs_ref[...] = _silu(proj(0)).astype(BF16)
    fr = proj(1)
    log_sig = jnp.minimum(fr, 0.0) - jnp.log1p(jnp.exp(-jnp.abs(fr)))
    a = loglb_ref[...]
    b = log1mlb_ref[...] + log_sig
    logf_ref[...] = jnp.maximum(a, b) + jnp.log1p(jnp.exp(-jnp.abs(a - b)))
    kk_ref[...] = (oneml_ref[...] * _sigmoid(-fr)).astype(BF16)
    v_ref[...] = proj(2).astype(BF16)
    ogs_ref[...] = _silu(proj(3)).astype(BF16)


def _inproj_hgrn(x, sc, sh, nw, w, loglb, log1mlb, oneml):
    n = x.shape[0]
    vec = _full((1, HG_DIM))
    out = lambda dt: jax.ShapeDtypeStruct((n, HG_DIM), dt)
    return pl.pallas_call(
        _inproj_hgrn_kernel,
        grid=(n // ROW_TILE,),
        in_specs=[_row_spec(D_MODEL), _mod_spec(), _mod_spec(), vec, _full(w.shape), vec, vec, vec],
        out_specs=[_row_spec(HG_DIM)] * 5,
        out_shape=[out(BF16), out(BF16), out(F32), out(BF16), out(BF16)],
        compiler_params=_params("arbitrary"),
        name="inproj_hgrn",
    )(x, sc, sh, nw, w, loglb, log1mlb, oneml)


def _ssd_kernel(zs_ref, xbc_ref, dt_ref, a_ref, cw_ref, cb_ref, dsk_ref, nw_ref, tril_ref, e_ref, eye_ref,
                *rest, L, has_init):
    if has_init:
        cs0_ref, s0_ref = rest[:2]
    yn_ref, cso_ref, so_ref, s_scr, prev_scr = rest[-5:]
    c = pl.program_id(1)

    @pl.when(c == 0)
    def _():
        prev_scr[...] = jnp.zeros(prev_scr.shape, F32)
        if has_init:
            s_scr[...] = s0_ref[0, 0]
            prev_scr[8 - (SSM_CONV - 1):8, :] = cs0_ref[0, 0]
        else:
            s_scr[...] = jnp.zeros(s_scr.shape, F32)

    u = xbc_ref[...]
    ext = jnp.concatenate([prev_scr[...], u], axis=0).reshape(L // 8 + 1, 8, SSM_CONV_DIM)
    row_in_tile = lax.broadcasted_iota(I32, (L // 8, 8, SSM_CONV_DIM), 1)
    conv = cb_ref[...] + cw_ref[SSM_CONV - 1:SSM_CONV, :] * u
    for j in range(1, SSM_CONV):
        rot = pltpu.roll(ext, j, axis=1)
        shifted = jnp.where(row_in_tile < j, rot[:L // 8], rot[1:]).reshape(L, SSM_CONV_DIM)
        conv = conv + cw_ref[SSM_CONV - 1 - j:SSM_CONV - j, :] * shifted
    prev_scr[...] = u[L - 8:L, :]
    cso_ref[0, 0] = u[L - (SSM_CONV - 1):L, :]
    xbc = _silu(conv)
    xs = xbc[:, :SSM_INNER]

    a = a_ref[...]
    acs = _dot_hi(tril_ref[...], a)
    acs_t = _dot_nt(eye_ref[...], acs, precision=HI)
    last = acs[L - 1:L, :]
    dt = dt_ref[...]
    per_head = jnp.concatenate([dt, jnp.exp(acs), dt * jnp.exp(last - acs)], axis=0)
    ph_hi = per_head.astype(BF16)
    ph_lo = (per_head - ph_hi.astype(F32)).astype(BF16)
    per_chan = _dot(ph_hi, e_ref[...]) + _dot(ph_lo, e_ref[...])
    dt_e, eacs_e, dtwj_e = per_chan[:L], per_chan[L:2 * L], per_chan[2 * L:]
    xdt_b = (xs * dt_e).astype(BF16)
    xw_b = (xs * dtwj_e).astype(BF16)
    e_last = jnp.broadcast_to(jnp.exp(acs_t[:, L - 1:L]), (SSM_HEADS, SSM_STATE))

    ii = lax.broadcasted_iota(I32, (L, L), 0)
    jj = lax.broadcasted_iota(I32, (L, L), 1)
    causal = ii >= jj
    first_head = lax.broadcasted_iota(I32, (L, 2 * SSM_HEAD_DIM), 1) < SSM_HEAD_DIM

    y_intra, y_state = [], []
    for g in range(SSM_GROUPS):
        b_g = xbc[:, SSM_INNER + g * SSM_STATE:SSM_INNER + (g + 1) * SSM_STATE].astype(BF16)
        c_off = SSM_INNER + SSM_GROUPS * SSM_STATE
        c_g = xbc[:, c_off + g * SSM_STATE:c_off + (g + 1) * SSM_STATE].astype(BF16)
        scores = _dot_nt(c_g, b_g)
        rows = slice(g * SSM_GROUP_WIDTH, (g + 1) * SSM_GROUP_WIDTH)
        s_g = s_scr[rows, :]
        y_state.append(_dot_nt(c_g, s_g.astype(BF16)))
        for pair in range(SSM_GROUP_WIDTH // (2 * SSM_HEAD_DIM)):
            h0 = g * (SSM_HEADS // SSM_GROUPS) + 2 * pair
            ms = []
            for h in (h0, h0 + 1):
                decay = jnp.where(causal, jnp.exp(acs[:, h:h + 1] - acs_t[h:h + 1, :]), 0.0)
                ms.append(scores * decay)
            m_pair = jnp.concatenate(ms, axis=1).astype(BF16)
            x_pair = xdt_b[:, h0 * SSM_HEAD_DIM:(h0 + 2) * SSM_HEAD_DIM]
            zero = jnp.zeros_like(x_pair)
            x_bd = jnp.concatenate([jnp.where(first_head, x_pair, zero), jnp.where(first_head, zero, x_pair)], axis=0)
            y_intra.append(_dot(m_pair, x_bd))
        heads = range(g * (SSM_HEADS // SSM_GROUPS), (g + 1) * (SSM_HEADS // SSM_GROUPS))
        row_scale = jnp.concatenate(
            [jnp.broadcast_to(e_last[h:h + 1, :], (SSM_HEAD_DIM, SSM_STATE)) for h in heads], axis=0)
        s_scr[rows, :] = s_g * row_scale + _dot_tn(xw_b[:, rows], b_g)

    y = jnp.concatenate(y_intra, axis=1) + jnp.concatenate(y_state, axis=1) * eacs_e + xs * dsk_ref[...]
    yz = y * zs_ref[...].astype(F32)
    normed = []
    for g in range(SSM_GROUPS):
        t = yz[:, g * SSM_GROUP_WIDTH:(g + 1) * SSM_GROUP_WIDTH]
        normed.append(t * lax.rsqrt(jnp.mean(t * t, axis=-1, keepdims=True) + EPS))
    yn_ref[...] = (jnp.concatenate(normed, axis=1) * nw_ref[...]).astype(BF16)

    @pl.when(c == pl.num_programs(1) - 1)
    def _():
        so_ref[0, 0] = s_scr[...]


def _alias_previous(prev, args, in_specs):
    aliases = {}
    for k, p in enumerate(prev):
        if p is not None:
            aliases[len(args)] = k
            args.append(p)
            in_specs.append(pl.BlockSpec(memory_space=pl.ANY))
    return aliases


def _ssd(zs, xbc, dt, a, cw, cb, dsk_e, nw, *, row0, nb, T, L, depth, layer, init, prev):
    n = zs.shape[0]
    nc = T // L
    blk0 = row0 // L
    row = lambda cols: pl.BlockSpec((L, cols), lambda b, c: (blk0 + b * nc + c, 0))
    tril = jnp.tril(jnp.ones((L, L), F32))
    expand = jnp.repeat(jnp.eye(SSM_HEADS, dtype=BF16), SSM_HEAD_DIM, axis=1)
    eye = jnp.eye(SSM_HEADS, dtype=F32)
    args = [zs, xbc, dt, a, cw, cb, dsk_e, nw, tril, expand, eye]
    in_specs = [row(SSM_INNER), row(SSM_CONV_DIM), row(SSM_HEADS), row(SSM_HEADS),
                _full(cw.shape), _full(cb.shape), _full(dsk_e.shape), _full(nw.shape),
                _full(tril.shape), _full(expand.shape), _full(eye.shape)]
    if init is not None:
        args += [init[0], init[1]]
        in_specs += [pl.BlockSpec((1, 1, SSM_CONV - 1, SSM_CONV_DIM), lambda b, c: (layer, b, 0, 0)),
                     pl.BlockSpec((1, 1, SSM_INNER, SSM_STATE), lambda b, c: (layer, b, 0, 0))]
    out_shape = [jax.ShapeDtypeStruct((n, SSM_INNER), BF16),
                 jax.ShapeDtypeStruct((depth, nb, SSM_CONV - 1, SSM_CONV_DIM), F32),
                 jax.ShapeDtypeStruct((depth, nb, SSM_INNER, SSM_STATE), F32)]
    aliases = _alias_previous(prev, args, in_specs)
    return pl.pallas_call(
        functools.partial(_ssd_kernel, L=L, has_init=init is not None),
        grid=(nb, nc),
        in_specs=in_specs,
        out_specs=[row(SSM_INNER),
                   pl.BlockSpec((1, 1, SSM_CONV - 1, SSM_CONV_DIM), lambda b, c: (layer, b, 0, 0)),
                   pl.BlockSpec((1, 1, SSM_INNER, SSM_STATE), lambda b, c: (layer, b, 0, 0))],
        out_shape=out_shape,
        scratch_shapes=[pltpu.VMEM((SSM_INNER, SSM_STATE), F32), pltpu.VMEM((8, SSM_CONV_DIM), F32)],
        input_output_aliases=aliases,
        compiler_params=_params("arbitrary", "arbitrary"),
        name=f"ssd_L{L}",
    )(*args)


def _hgrn_levels(C):
    s, out = C // 2, []
    while s >= HG_BAND:
        out.append(s)
        s //= 2
    return out


def _shift_in_block(x, d):
    if d == 0:
        return x
    n, w = x.shape
    return pltpu.roll(x.reshape(n // HG_BAND, HG_BAND, w), d, axis=1).reshape(n, w)


def _hgrn_kernel(q_ref, k_ref, v_ref, g_ref, og_ref, nw_ref, tril_ref, ones_ref, *rest, C, has_init):
    if has_init:
        s0_ref = rest[0]
    o_ref, so_ref, st_scr, pbuf, rbuf = rest[-5:]
    c = pl.program_id(1)
    K = HG_HEAD_DIM

    @pl.when(c == 0)
    def _():
        for h in range(HG_HEADS):
            st_scr[h] = s0_ref[0, 0, h].T if has_init else jnp.zeros((K, K), F32)

    q = q_ref[...].astype(F32)
    k = k_ref[...].astype(F32)
    v = v_ref[...].astype(F32)
    gcs = _dot_hi(tril_ref[...], g_ref[...])

    row_in_block = lax.broadcasted_iota(I32, (C, HG_DIM), 0) % HG_BAND
    for d in range(HG_BAND):
        ks = _shift_in_block(k, d)
        gs = _shift_in_block(gcs, d)
        e = jnp.where(row_in_block >= d, jnp.exp(gcs - gs), 0.0)
        p = (q * ks * e).astype(BF16)
        for h in range(HG_HEADS):
            r0 = (d * HG_HEADS + h) * C
            pbuf[r0:r0 + C, :] = p[:, h * K:(h + 1) * K]
    rbuf[...] = _dot(pbuf[...], ones_ref[...])

    ii = lax.broadcasted_iota(I32, (C, C), 0)
    jj = lax.broadcasted_iota(I32, (C, C), 1)
    levels = _hgrn_levels(C)
    att = [jnp.zeros((C, C), F32) for _ in range(HG_HEADS)]
    for s in levels:
        q_parts, k_parts = [], []
        zero = jnp.zeros((s, HG_DIM), F32)
        for lo in range(0, C, 2 * s):
            mid = lo + s
            ref_row = gcs[mid - 1:mid, :]
            q_parts += [zero, q[mid:mid + s] * jnp.exp(gcs[mid:mid + s] - ref_row)]
            k_parts += [k[lo:mid] * jnp.exp(ref_row - gcs[lo:mid]), zero]
        qt = jnp.concatenate(q_parts, axis=0).astype(BF16)
        kt = jnp.concatenate(k_parts, axis=0).astype(BF16)
        same_segment = (ii // (2 * s)) == (jj // (2 * s))
        for h in range(HG_HEADS):
            att[h] = att[h] + jnp.where(same_segment, _dot_nt(qt[:, h * K:(h + 1) * K], kt[:, h * K:(h + 1) * K]), 0.0)

    last = gcs[C - 1:C, :]
    q_state = (q * jnp.exp(gcs)).astype(BF16)
    k_last = (k * jnp.exp(last - gcs)).astype(BF16)
    e_last = jnp.exp(last)
    v_b = v_ref[...]
    outs = []
    for h in range(HG_HEADS):
        cols = slice(h * K, (h + 1) * K)
        st = st_scr[h]
        o = _dot_nt(q_state[:, cols], st.astype(BF16))
        if levels:
            o = o + _dot(att[h].astype(BF16), v_b[:, cols])
        for d in range(HG_BAND):
            r0 = (d * HG_HEADS + h) * C
            o = o + rbuf[r0:r0 + C, :] * _shift_in_block(v[:, cols], d)
        st_scr[h] = st * e_last[:, cols] + _dot_tn(v_b[:, cols], k_last[:, cols])
        on = o * lax.rsqrt(jnp.mean(o * o, axis=-1, keepdims=True) + EPS)
        outs.append(on)
    o_all = jnp.concatenate(outs, axis=1) * nw_ref[...] * og_ref[...].astype(F32)
    o_ref[...] = o_all.astype(BF16)

    @pl.when(c == pl.num_programs(1) - 1)
    def _():
        for h in range(HG_HEADS):
            so_ref[0, 0, h] = st_scr[h].T


def _hgrn(qs, kk, v, logf, ogs, nw_e, *, row0, nb, T, C, depth, layer, init, prev):
    n = qs.shape[0]
    nc = T // C
    blk0 = row0 // C
    row = pl.BlockSpec((C, HG_DIM), lambda b, c: (blk0 + b * nc + c, 0))
    K = HG_HEAD_DIM
    tril = jnp.tril(jnp.ones((C, C), F32))
    ones = jnp.ones((K, K), BF16)
    args = [qs, kk, v, logf, ogs, nw_e, tril, ones]
    in_specs = [row, row, row, row, row, _full(nw_e.shape), _full(tril.shape), _full(ones.shape)]
    state_spec = pl.BlockSpec((1, 1, HG_HEADS, K, K), lambda b, c: (layer, b, 0, 0, 0))
    if init is not None:
        args.append(init)
        in_specs.append(state_spec)
    out_shape = [jax.ShapeDtypeStruct((n, HG_DIM), BF16),
                 jax.ShapeDtypeStruct((depth, nb, HG_HEADS, K, K), F32)]
    aliases = _alias_previous(prev, args, in_specs)
    band_rows = HG_BAND * HG_HEADS * C
    return pl.pallas_call(
        functools.partial(_hgrn_kernel, C=C, has_init=init is not None),
        grid=(nb, nc),
        in_specs=in_specs,
        out_specs=[row, state_spec],
        out_shape=out_shape,
        scratch_shapes=[pltpu.VMEM((HG_HEADS, K, K), F32),
                        pltpu.VMEM((band_rows, K), BF16), pltpu.VMEM((band_rows, K), F32)],
        input_output_aliases=aliases,
        compiler_params=_params("arbitrary", "arbitrary"),
        name=f"hgrn_C{C}",
    )(*args)


def _merge_kernel(yn_ref, o_ref, x_ref, sc1_ref, sh1_ref, nw1_ref, wg_ref, g1_ref, sc2_ref, sh2_ref, nw2_ref,
                  wbs_ref, wbh_ref, wout_ref, rwt_ref, rb_ref, before_ref,
                  xo_ref, h2_ref, idx_ref, p_ref, rank_ref, cnt_ref, cnt_scr):
    hb = _mod_norm(x_ref[...], nw1_ref[...], sc1_ref[...], sh1_ref[...]).astype(BF16)
    gate_ssm = _sigmoid(_dot(hb, wg_ref[:, :D_MODEL]))
    gate_hgrn = _sigmoid(_dot(hb, wg_ref[:, D_MODEL:]))
    m = gate_ssm * _dot(yn_ref[...], wbs_ref[...]) + gate_hgrn * _dot(o_ref[...], wbh_ref[...])
    mo = _dot(m.astype(BF16), wout_ref[...])
    x3 = x_ref[...].reshape(ROW_TILE // MOD_ROWS, MOD_ROWS, D_MODEL)
    mo3 = mo.reshape(ROW_TILE // MOD_ROWS, MOD_ROWS, D_MODEL)
    xn = (x3 + g1_ref[...] * mo3).reshape(ROW_TILE, D_MODEL)
    xo_ref[...] = xn
    h2 = _mod_norm(xn, nw2_ref[...], sc2_ref[...], sh2_ref[...])
    _to_token_tiles(h2_ref, 0, h2)
    h2_hi = h2.astype(BF16)
    h2_lo = (h2 - h2_hi.astype(F32)).astype(BF16)
    logits = _dot_nt(rwt_ref[...], h2_hi) + _dot_nt(rwt_ref[...], h2_lo) + rb_ref[...]
    expert = lax.broadcasted_iota(I32, logits.shape, 0)
    vals, idxs, chosen = [], [], []
    for _ in range(TOP_K):
        mx = jnp.max(logits, axis=0, keepdims=True)
        first = jnp.min(jnp.where(logits == mx, expert, N_EXPERTS), axis=0, keepdims=True)
        sel = expert == first
        vals.append(mx)
        idxs.append(first)
        chosen.append(sel)
        logits = jnp.where(sel, -jnp.inf, logits)
    ev = [jnp.exp(v - vals[0]) for v in vals]
    den = ev[0] + ev[1] + ev[2] + ev[3]
    p_ref[...] = jnp.concatenate([e / den for e in ev] + [jnp.zeros((8 - TOP_K, ROW_TILE), F32)], axis=0)
    idx_ref[...] = jnp.concatenate(idxs, axis=0)

    @pl.when(pl.program_id(0) == 0)
    def _():
        cnt_scr[...] = jnp.zeros(cnt_scr.shape, F32)

    member = jnp.zeros(logits.shape, F32)
    for sel in chosen:
        member = member + sel.astype(F32)
    base = cnt_scr[...] + _dot(member.astype(BF16), before_ref[...])
    rank_ref[...] = jnp.concatenate(
        [jnp.sum(jnp.where(sel, base, 0.0), axis=0, keepdims=True) for sel in chosen], axis=0).astype(I32)
    cnt_scr[...] = cnt_scr[...] + jnp.sum(member, axis=1, keepdims=True)
    cnt_ref[...] = cnt_scr[:, :128]


def _merge(yn, o, x, sc1, sh1, nw1, wg, g1, sc2, sh2, nw2, wbs, wbh, wout, rwt, rb):
    n = x.shape[0]
    col_spec = lambda rows: pl.BlockSpec((rows, ROW_TILE), lambda i: (0, i))
    before = jnp.triu(jnp.ones((ROW_TILE, ROW_TILE), BF16), k=1)
    return pl.pallas_call(
        _merge_kernel,
        grid=(n // ROW_TILE,),
        in_specs=[_row_spec(SSM_INNER), _row_spec(HG_DIM), _row_spec(D_MODEL),
                  _mod_spec(), _mod_spec(), _full((1, D_MODEL)), _full(wg.shape),
                  _mod_spec(), _mod_spec(), _mod_spec(), _full((1, D_MODEL)),
                  _full(wbs.shape), _full(wbh.shape), _full(wout.shape), _full(rwt.shape), _full((N_EXPERTS, 1)),
                  _full(before.shape)],
        out_specs=[_row_spec(D_MODEL), pl.BlockSpec((ROW_TILE * TOKEN_TILE_ROWS, 128), lambda i: (i, 0)),
                   col_spec(TOP_K), col_spec(8), col_spec(TOP_K), _full((N_EXPERTS, 128))],
        out_shape=[jax.ShapeDtypeStruct((n, D_MODEL), F32), jax.ShapeDtypeStruct((n * TOKEN_TILE_ROWS, 128), F32),
                   jax.ShapeDtypeStruct((TOP_K, n), I32), jax.ShapeDtypeStruct((8, n), F32),
                   jax.ShapeDtypeStruct((TOP_K, n), I32), jax.ShapeDtypeStruct((N_EXPERTS, 128), F32)],
        scratch_shapes=[pltpu.VMEM((N_EXPERTS, ROW_TILE), F32)],
        compiler_params=_params("arbitrary"),
        name="merge_router",
    )(yn, o, x, sc1, sh1, nw1, wg, g1, sc2, sh2, nw2, wbs, wbh, wout, rwt, rb, before)


TOKEN_TILE_ROWS = D_MODEL // 128


def _to_token_tiles(ref, base, x):
    for c in range(TOKEN_TILE_ROWS):
        ref[pl.ds(base + c, x.shape[0], stride=TOKEN_TILE_ROWS), :] = x[:, c * 128:(c + 1) * 128]


def _from_token_tiles(ref, base, n):
    return jnp.concatenate([ref[pl.ds(base + c, n, stride=TOKEN_TILE_ROWS), :] for c in range(TOKEN_TILE_ROWS)],
                           axis=1)


def _row_copy(src, src_row, dst, dst_row, sem):
    def first(row):
        start = row * TOKEN_TILE_ROWS
        return start if isinstance(start, int) else pl.multiple_of(start, TOKEN_TILE_ROWS)
    return pltpu.make_async_copy(src.at[pl.ds(first(src_row), TOKEN_TILE_ROWS), :],
                                 dst.at[pl.ds(first(dst_row), TOKEN_TILE_ROWS), :], sem)


def _start_rows(idx_ref, copy_of_row, unrolled):
    if unrolled:
        for r in range(EXPERT_TILE):
            copy_of_row(r, idx_ref[0, 0, r]).start()
    else:
        def body(r, carry):
            copy_of_row(r, idx_ref[0, 0, r]).start()
            return carry
        lax.fori_loop(0, EXPERT_TILE, body, 0, unroll=8)


def _gmm_kernel(te_ref, src_ref, src_next_ref, dst_ref, dst_prev_ref, h_hbm, wgu_ref, bgu_ref, wd_ref, bd_ref,
                y_hbm, xbuf, ybuf, gsem, ssem, wgu_b, wd_b):
    i = pl.program_id(0)
    last = pl.num_programs(0) - 1
    slot = i % 2
    half_rows = EXPERT_TILE * TOKEN_TILE_ROWS

    def half(buf_ref, buf):
        return buf_ref.at[pl.ds(pl.multiple_of(buf * half_rows, half_rows), half_rows), :]

    def gather(buf):
        return lambda r, row: _row_copy(h_hbm, row, half(xbuf, buf), r, gsem.at[buf])

    def scatter(buf):
        return lambda r, row: _row_copy(half(ybuf, buf), r, y_hbm, row, ssem.at[0])

    def wait_gather(buf):
        pltpu.make_async_copy(h_hbm.at[pl.ds(0, half_rows), :], half(xbuf, buf), gsem.at[buf]).wait()

    def wait_scatter(buf):
        pltpu.make_async_copy(half(ybuf, buf), y_hbm.at[pl.ds(0, half_rows), :], ssem.at[0]).wait()

    @pl.when(i == 0)
    def _():
        ybuf[...] = jnp.zeros(ybuf.shape, F32)
        spare = y_hbm.at[pl.ds(y_hbm.shape[0] - 2 * half_rows, 2 * half_rows), :]
        zero_spare = pltpu.make_async_copy(ybuf, spare, ssem.at[0])
        zero_spare.start()
        zero_spare.wait()
        _start_rows(src_ref, gather(0), unrolled=False)

    @pl.when(i > 0)
    def _():
        wait_scatter(slot)

    @pl.when(jnp.logical_or(i == 0, te_ref[i] != te_ref[jnp.maximum(i - 1, 0)]))
    def _():
        wgu_b[...] = wgu_ref[0].astype(BF16)
        wd_b[...] = wd_ref[0].astype(BF16)

    wait_gather(slot)
    base = pl.multiple_of(slot * half_rows, half_rows)
    x = _from_token_tiles(xbuf, base, EXPERT_TILE).astype(BF16)
    _start_rows(dst_prev_ref, scatter(1 - slot), unrolled=True)
    _start_rows(src_next_ref, gather(1 - slot), unrolled=True)
    gu = _dot(x, wgu_b[...]) + bgu_ref[0]
    gate = jnp.minimum(gu[:, :D_FF], SWIGLU_LIMIT)
    up = jnp.clip(gu[:, D_FF:], -SWIGLU_LIMIT, SWIGLU_LIMIT)
    act = gate * _sigmoid(SWIGLU_ALPHA * gate) * (up + 1.0)
    _to_token_tiles(ybuf, base, _dot(act.astype(BF16), wd_b[...]) + bd_ref[0])

    @pl.when(i == last)
    def _():
        wait_scatter(1 - slot)
        _start_rows(dst_ref, scatter(slot), unrolled=False)
        wait_scatter(slot)
        wait_gather(1 - slot)


def _gmm(tile_expert, src, dst_ext, h2, wgu, bgu, wd, bd, layer, n_out_rows):
    n_tiles = tile_expert.shape[0]
    smem_idx = lambda f: pl.BlockSpec((1, 1, EXPERT_TILE), f, memory_space=pltpu.SMEM)
    weight = lambda shape: pl.BlockSpec((1, 1) + shape, lambda i, te: (layer, te[i], 0, 0))
    grid_spec = pltpu.PrefetchScalarGridSpec(
        num_scalar_prefetch=1,
        grid=(n_tiles,),
        in_specs=[smem_idx(lambda i, te: (i, 0, 0)),
                  smem_idx(lambda i, te: (jnp.minimum(i + 1, n_tiles - 1), 0, 0)),
                  smem_idx(lambda i, te: (i + 1, 0, 0)),
                  smem_idx(lambda i, te: (i, 0, 0)),
                  pl.BlockSpec(memory_space=pl.ANY),
                  weight((D_MODEL, 2 * D_FF)), weight((1, 2 * D_FF)), weight((D_FF, D_MODEL)), weight((1, D_MODEL))],
        out_specs=pl.BlockSpec(memory_space=pl.ANY),
        scratch_shapes=[pltpu.VMEM((2 * EXPERT_TILE * TOKEN_TILE_ROWS, 128), F32),
                        pltpu.VMEM((2 * EXPERT_TILE * TOKEN_TILE_ROWS, 128), F32),
                        pltpu.SemaphoreType.DMA((2,)), pltpu.SemaphoreType.DMA((1,)),
                        pltpu.VMEM((D_MODEL, 2 * D_FF), BF16), pltpu.VMEM((D_FF, D_MODEL), BF16)],
    )
    return pl.pallas_call(
        _gmm_kernel_4d,
        grid_spec=grid_spec,
        out_shape=jax.ShapeDtypeStruct((n_out_rows * TOKEN_TILE_ROWS, 128), F32),
        compiler_params=_params("arbitrary"),
        name="moe_gmm",
    )(tile_expert, src, src, dst_ext, dst_ext, h2, wgu, bgu.reshape(bgu.shape[0], bgu.shape[1], 1, -1), wd,
      bd.reshape(bd.shape[0], bd.shape[1], 1, -1))


def _gmm_kernel_4d(te_ref, src_ref, src_next_ref, dst_ref, dst_prev_ref, h_hbm, wgu_ref, bgu_ref, wd_ref, bd_ref,
                   *rest):
    return _gmm_kernel(te_ref, src_ref, src_next_ref, dst_ref, dst_prev_ref, h_hbm, wgu_ref.at[0], bgu_ref.at[0],
                       wd_ref.at[0], bd_ref.at[0], *rest)


def _combine_kernel(y0_ref, y1_ref, y2_ref, y3_ref, p_ref, eye_ref, x_ref, g2_ref, fw_ref, xo_ref, *, final):
    p = _dot_nt(eye_ref[...], p_ref[...], precision=HI)
    acc = jnp.zeros((ROW_TILE, D_MODEL), F32)
    for k, y_ref in enumerate((y0_ref, y1_ref, y2_ref, y3_ref)):
        acc = acc + p[:, k:k + 1] * _from_token_tiles(y_ref, 0, ROW_TILE)
    x3 = x_ref[...].reshape(ROW_TILE // MOD_ROWS, MOD_ROWS, D_MODEL)
    acc3 = acc.reshape(ROW_TILE // MOD_ROWS, MOD_ROWS, D_MODEL)
    xn = (x3 + g2_ref[...] * acc3).reshape(ROW_TILE, D_MODEL)
    if final:
        xn = xn * lax.rsqrt(jnp.mean(xn * xn, axis=-1, keepdims=True) + EPS) * fw_ref[...]
    xo_ref[...] = xn


def _combine(y_pairs, p_t, x, g2, fw, final):
    n = x.shape[0]
    n_tiles = n // ROW_TILE
    eye = jnp.eye(ROW_TILE, dtype=F32)
    y_spec = lambda k: pl.BlockSpec((ROW_TILE * TOKEN_TILE_ROWS, 128), lambda i: (k * n_tiles + i, 0))
    return pl.pallas_call(
        functools.partial(_combine_kernel, final=final),
        grid=(n_tiles,),
        in_specs=[y_spec(0), y_spec(1), y_spec(2), y_spec(3),
                  pl.BlockSpec((8, ROW_TILE), lambda i: (0, i)), _full(eye.shape),
                  _row_spec(D_MODEL), _mod_spec(), _full((1, D_MODEL))],
        out_specs=_row_spec(D_MODEL),
        out_shape=jax.ShapeDtypeStruct((n, D_MODEL), F32),
        compiler_params=_params("arbitrary"),
        name="moe_combine",
    )(y_pairs, y_pairs, y_pairs, y_pairs, p_t, eye, x, g2, fw)


def _route(idx_t, rank_t, counts):
    n = idx_t.shape[1]
    n_pairs = TOP_K * n
    n_tiles = n_pairs // EXPERT_TILE + N_EXPERTS
    counts = counts[:, 0].astype(I32)
    padded = (counts + EXPERT_TILE - 1) // EXPERT_TILE * EXPERT_TILE
    ends = jnp.cumsum(padded)
    starts = ends - padded
    experts = jnp.arange(N_EXPERTS, dtype=I32)
    start_of = jnp.sum(jnp.where(idx_t[..., None] == experts, starts, 0), axis=-1)
    slots = (start_of + rank_t).reshape(-1)
    pair = jnp.full((n_tiles * EXPERT_TILE,), -1, I32).at[slots].set(jnp.arange(n_pairs, dtype=I32))
    row = jnp.arange(n_tiles * EXPERT_TILE, dtype=I32)
    spare = n_pairs + ((row // EXPERT_TILE) % 2) * EXPERT_TILE + row % EXPERT_TILE
    src = jnp.where(pair >= 0, pair % n, 0).reshape(n_tiles, 1, EXPERT_TILE)
    dst = jnp.where(pair >= 0, pair, spare)
    before_first = n_pairs + EXPERT_TILE + jnp.arange(EXPERT_TILE, dtype=I32)
    dst_ext = jnp.concatenate([before_first, dst]).reshape(n_tiles + 1, 1, EXPERT_TILE)
    tile_start = jnp.arange(n_tiles, dtype=I32) * EXPERT_TILE
    te = jnp.minimum(jnp.searchsorted(ends, tile_start, side="right").astype(I32), N_EXPERTS - 1)
    last_used = te[jnp.maximum(ends[-1] // EXPERT_TILE - 1, 0)]
    te = jnp.where(tile_start < ends[-1], te, last_used)
    return src, dst_ext, te, n_pairs + 2 * EXPERT_TILE


def kernel(x_prompt, x_sample, c_prompt, c_sample, state_ssm, state_conv, state_hgrn, norm_mix_w, norm_ffn_w, ada_w, ada_b, w_in, conv_w, conv_b, dt_bias, a_log, d_skip, ssm_norm_w, hgrn_lb_logits, hgrn_norm_w, w_branch_ssm, w_branch_hgrn, w_out, router_w, router_b, w_gate_up, b_gate_up, w_down, b_down, final_norm_w):
    nbp, tp, _ = x_prompt.shape
    nbs, ts, _ = x_sample.shape
    depth = w_in.shape[0]
    n_p, n_s = nbp * tp, nbs * ts
    assert depth == 2 and tp % ROW_TILE == 0 and n_s % ROW_TILE == 0 and ts == MOD_ROWS
    l_p, l_s = math.gcd(tp, SSM_CHUNK), math.gcd(ts, SSM_CHUNK)
    c_p, c_s = math.gcd(tp, HG_CHUNK), math.gcd(ts, HG_CHUNK)
    assert c_s == HG_BAND and c_p % HG_BAND == 0

    x = jnp.concatenate([x_prompt.reshape(n_p, D_MODEL), x_sample.reshape(n_s, D_MODEL)], axis=0)
    c_all = jnp.concatenate([c_prompt, c_sample], axis=0)

    lb_all = jnp.cumsum(jax.nn.softmax(hgrn_lb_logits.astype(F32), axis=0), axis=0)
    lb_all = lb_all - lb_all[:1]

    state_ssm2 = state_ssm.reshape(depth, nbs, SSM_INNER, SSM_STATE)
    offs = [0, 2048, 5120, 5152, 6176, 7200, 8224, 9248, 10272, 11296]

    ssd_p = ssd_s = [None, None, None]
    hg_p = hg_s = [None, None]
    for l in range(depth):
        mod = _ada(c_all, ada_w[l], ada_b[l])
        mod_rows = jnp.concatenate([jnp.repeat(mod[:nbp], tp // MOD_ROWS, axis=0), mod[nbp:]], axis=0)
        sh1, sc1, g1, sh2, sc2, g2 = [m.reshape(-1, 1, D_MODEL) for m in jnp.split(mod_rows, 6, axis=-1)]

        wl = w_in[l]
        wz = wl[:, offs[0]:offs[1]].astype(BF16)
        wx = wl[:, offs[1]:offs[2]].astype(BF16)
        wdt = wl[:, offs[2]:offs[3]]
        whg = wl[:, offs[3]:offs[7]].astype(BF16)
        wgates = wl[:, offs[7]:].astype(BF16)
        nw1 = norm_mix_w[l].reshape(1, -1)

        zs, xbc, dt, a = _inproj_ssd(x, sc1, sh1, nw1, wz, wx, wdt, dt_bias[l].reshape(1, -1), a_log[l].reshape(1, -1))
        lb = lb_all[l].reshape(1, -1)
        qs, kk, logf, v, ogs = _inproj_hgrn(x, sc1, sh1, nw1, whg, jnp.log(lb), jnp.log1p(-lb), 1.0 - lb)

        ssd_args = (zs, xbc, dt, a, conv_w[l], conv_b[l].reshape(1, -1),
                    jnp.repeat(d_skip[l], SSM_HEAD_DIM).reshape(1, -1), ssm_norm_w[l].reshape(1, -1))
        ssd_p = _ssd(*ssd_args, row0=0, nb=nbp, T=tp, L=l_p, depth=depth, layer=l, init=None,
                     prev=[None] + list(ssd_p[1:]))
        ssd_s = _ssd(*ssd_args, row0=n_p, nb=nbs, T=ts, L=l_s, depth=depth, layer=l, init=(state_conv, state_ssm2),
                     prev=[ssd_p[0]] + list(ssd_s[1:]))
        yn = ssd_s[0]

        hg_args = (qs, kk, v, logf, ogs, jnp.tile(hgrn_norm_w[l], HG_HEADS).reshape(1, -1))
        hg_p = _hgrn(*hg_args, row0=0, nb=nbp, T=tp, C=c_p, depth=depth, layer=l, init=None,
                     prev=[None] + list(hg_p[1:]))
        hg_s = _hgrn(*hg_args, row0=n_p, nb=nbs, T=ts, C=c_s, depth=depth, layer=l, init=state_hgrn,
                     prev=[hg_p[0]] + list(hg_s[1:]))
        o = hg_s[0]

        x1, h2, idx_t, p_t, rank_t, counts = _merge(
            yn, o, x, sc1, sh1, nw1, wgates, g1, sc2, sh2, norm_ffn_w[l].reshape(1, -1),
            w_branch_ssm[l].astype(BF16), w_branch_hgrn[l].astype(BF16), w_out[l].astype(BF16),
            router_w[l].T.astype(BF16), router_b[l].reshape(-1, 1))
        src, dst_ext, te, n_out_rows = _route(idx_t, rank_t, counts)
        y_pairs = _gmm(te, src, dst_ext, h2, w_gate_up, b_gate_up, w_down, b_down, l, n_out_rows)
        x = _combine(y_pairs, p_t, x1, g2, final_norm_w.reshape(1, -1), final=(l == depth - 1))

    y_prompt = x[:n_p].reshape(nbp, tp, D_MODEL)
    y_sample = x[n_p:].reshape(nbs, ts, D_MODEL)
    ssm_p = ssd_p[2].reshape(depth, nbp, SSM_HEADS, SSM_HEAD_DIM, SSM_STATE)
    ssm_s = ssd_s[2].reshape(depth, nbs, SSM_HEADS, SSM_HEAD_DIM, SSM_STATE)
    return (y_prompt, y_sample, ssm_p, ssd_p[1], hg_p[1], ssm_s, ssd_s[1], hg_s[1])
```

```python
import functools
import math

import jax
import jax.numpy as jnp
from jax import lax
from jax.experimental import pallas as pl
from jax.experimental.pallas import tpu as pltpu

F32 = jnp.float32
BF16 = jnp.bfloat16
I32 = jnp.int32
HI = lax.Precision.HIGHEST

D_MODEL = 1024
SSM_INNER = 2048
SSM_HEAD_DIM = 64
SSM_HEADS = 32
SSM_GROUPS = 4
SSM_GROUP_WIDTH = SSM_INNER // SSM_GROUPS
SSM_STATE = 128
SSM_CONV = 4
SSM_CONV_DIM = 3072
SSM_CHUNK = 128
HG_DIM = 1024
HG_HEAD_DIM = 128
HG_HEADS = 8
HG_CHUNK = 64
HG_BAND = 8
N_EXPERTS = 32
TOP_K = 4
D_FF = 1024
SWIGLU_LIMIT = 7.0
SWIGLU_ALPHA = 1.702
EPS = 1e-6

ROW_TILE = 256
MOD_ROWS = 8
EXPERT_TILE = 256
VMEM_LIMIT_V7X = 56 * 1024 * 1024

_NT = (((1,), (1,)), ((), ()))
_TN = (((0,), (0,)), ((), ()))


def _params(*sem):
    return pltpu.CompilerParams(dimension_semantics=sem, vmem_limit_bytes=VMEM_LIMIT_V7X)


def _dot(a, b):
    return jnp.dot(a, b, preferred_element_type=F32)


def _dot_hi(a, b):
    return jnp.dot(a, b, precision=HI, preferred_element_type=F32)


def _dot_nt(a, b, precision=None):
    return lax.dot_general(a, b, _NT, precision=precision, preferred_element_type=F32)


def _dot_tn(a, b):
    return lax.dot_general(a, b, _TN, preferred_element_type=F32)


def _sigmoid(x):
    return 1.0 / (1.0 + jnp.exp(-x))


def _silu(x):
    return x * _sigmoid(x)


def _softplus(x):
    return jnp.maximum(x, 0.0) + jnp.log1p(jnp.exp(-jnp.abs(x)))


def _full(shape):
    nd = len(shape)
    return pl.BlockSpec(shape, lambda *_: (0,) * nd)


def _ada_kernel(c_ref, w_ref, b_ref, o_ref):
    o_ref[...] = _dot_hi(_silu(c_ref[...]), w_ref[...]) + b_ref[...]


def _ada(c_all, w, b):
    n = c_all.shape[0]
    tn = 1536
    return pl.pallas_call(
        _ada_kernel,
        grid=(6 * D_MODEL // tn,),
        in_specs=[_full((n, D_MODEL)),
                  pl.BlockSpec((D_MODEL, tn), lambda j: (0, j)),
                  pl.BlockSpec((1, tn), lambda j: (0, j))],
        out_specs=pl.BlockSpec((n, tn), lambda j: (0, j)),
        out_shape=jax.ShapeDtypeStruct((n, 6 * D_MODEL), F32),
        compiler_params=_params("arbitrary"),
        name="ada",
    )(c_all, w, b.reshape(1, -1))


def _mod_norm(x, nw, sc, sh):
    ms = jnp.mean(x * x, axis=-1, keepdims=True)
    y = x * lax.rsqrt(ms + EPS) * nw
    y3 = y.reshape(ROW_TILE // MOD_ROWS, MOD_ROWS, D_MODEL)
    return (y3 * (1.0 + sc) + sh).reshape(ROW_TILE, D_MODEL)


def _mod_spec():
    return pl.BlockSpec((ROW_TILE // MOD_ROWS, 1, D_MODEL), lambda i: (i, 0, 0))


def _row_spec(cols):
    return pl.BlockSpec((ROW_TILE, cols), lambda i: (i, 0))


def _inproj_ssd_kernel(x_ref, sc_ref, sh_ref, nw_ref, wz_ref, wx_ref, wdt_ref, dtb_ref, alog_ref,
                       zs_ref, xbc_ref, dt_ref, a_ref):
    h = _mod_norm(x_ref[...], nw_ref[...], sc_ref[...], sh_ref[...])
    hb = h.astype(BF16)
    zs_ref[...] = _silu(_dot(hb, wz_ref[...])).astype(BF16)
    xbc_ref[...] = _dot(hb, wx_ref[...])
    dt = _softplus(_dot_hi(h, wdt_ref[...]) + dtb_ref[...])
    dt_ref[...] = dt
    a_ref[...] = -jnp.exp(alog_ref[...]) * dt


def _inproj_ssd(x, sc, sh, nw, wz, wx, wdt, dtb, alog):
    n = x.shape[0]
    return pl.pallas_call(
        _inproj_ssd_kernel,
        grid=(n // ROW_TILE,),
        in_specs=[_row_spec(D_MODEL), _mod_spec(), _mod_spec(), _full((1, D_MODEL)),
                  _full(wz.shape), _full(wx.shape), _full(wdt.shape), _full((1, SSM_HEADS)), _full((1, SSM_HEADS))],
        out_specs=[_row_spec(SSM_INNER), _row_spec(SSM_CONV_DIM), _row_spec(SSM_HEADS), _row_spec(SSM_HEADS)],
        out_shape=[jax.ShapeDtypeStruct((n, SSM_INNER), BF16), jax.ShapeDtypeStruct((n, SSM_CONV_DIM), F32),
                   jax.ShapeDtypeStruct((n, SSM_HEADS), F32), jax.ShapeDtypeStruct((n, SSM_HEADS), F32)],
        compiler_params=_params("arbitrary"),
        name="inproj_ssd",
    )(x, sc, sh, nw, wz, wx, wdt, dtb, alog)


def _inproj_hgrn_kernel(x_ref, sc_ref, sh_ref, nw_ref, w_ref, loglb_ref, log1mlb_ref, oneml_ref,
                        qs_ref, kk_ref, logf_ref, v_ref, ogs_ref):
    h = _mod_norm(x_ref[...], nw_ref[...], sc_ref[...], sh_ref[...])
    hb = h.astype(BF16)

    def proj(k):
        return _dot(hb, w_ref[:, k * D_MODEL:(k + 1) * D_MODEL])

    qs_ref[...] = _silu(proj(0)).astype(BF16)
    fr = proj(1)
    log_sig = jnp.minimum(fr, 0.0) - jnp.log1p(jnp.exp(-jnp.abs(fr)))
    a = loglb_ref[...]
    b = log1mlb_ref[...] + log_sig
    logf_ref[...] = jnp.maximum(a, b) + jnp.log1p(jnp.exp(-jnp.abs(a - b)))
    kk_ref[...] = (oneml_ref[...] * _sigmoid(-fr)).astype(BF16)
    v_ref[...] = proj(2).astype(BF16)
    ogs_ref[...] = _silu(proj(3)).astype(BF16)


def _inproj_hgrn(x, sc, sh, nw, w, loglb, log1mlb, oneml):
    n = x.shape[0]
    vec = _full((1, HG_DIM))
    out = lambda dt: jax.ShapeDtypeStruct((n, HG_DIM), dt)
    return pl.pallas_call(
        _inproj_hgrn_kernel,
        grid=(n // ROW_TILE,),
        in_specs=[_row_spec(D_MODEL), _mod_spec(), _mod_spec(), vec, _full(w.shape), vec, vec, vec],
        out_specs=[_row_spec(HG_DIM)] * 5,
        out_shape=[out(BF16), out(BF16), out(F32), out(BF16), out(BF16)],
        compiler_params=_params("arbitrary"),
        name="inproj_hgrn",
    )(x, sc, sh, nw, w, loglb, log1mlb, oneml)


def _ssd_kernel(zs_ref, xbc_ref, dt_ref, a_ref, cw_ref, cb_ref, dsk_ref, nw_ref, tril_ref, e_ref, eye_ref,
                *rest, L, has_init):
    if has_init:
        cs0_ref, s0_ref = rest[:2]
    yn_ref, cso_ref, so_ref, s_scr, prev_scr = rest[-5:]
    c = pl.program_id(1)

    @pl.when(c == 0)
    def _():
        prev_scr[...] = jnp.zeros(prev_scr.shape, F32)
        if has_init:
            s_scr[...] = s0_ref[0, 0]
            prev_scr[8 - (SSM_CONV - 1):8, :] = cs0_ref[0, 0]
        else:
            s_scr[...] = jnp.zeros(s_scr.shape, F32)

    u = xbc_ref[...]
    ext = jnp.concatenate([prev_scr[...], u], axis=0).reshape(L // 8 + 1, 8, SSM_CONV_DIM)
    row_in_tile = lax.broadcasted_iota(I32, (L // 8, 8, SSM_CONV_DIM), 1)
    conv = cb_ref[...] + cw_ref[SSM_CONV - 1:SSM_CONV, :] * u
    for j in range(1, SSM_CONV):
        rot = pltpu.roll(ext, j, axis=1)
        shifted = jnp.where(row_in_tile < j, rot[:L // 8], rot[1:]).reshape(L, SSM_CONV_DIM)
        conv = conv + cw_ref[SSM_CONV - 1 - j:SSM_CONV - j, :] * shifted
    prev_scr[...] = u[L - 8:L, :]
    cso_ref[0, 0] = u[L - (SSM_CONV - 1):L, :]
    xbc = _silu(conv)
    xs = xbc[:, :SSM_INNER]

    a = a_ref[...]
    acs = _dot_hi(tril_ref[...], a)
    acs_t = _dot_nt(eye_ref[...], acs, precision=HI)
    last = acs[L - 1:L, :]
    dt = dt_ref[...]
    per_head = jnp.concatenate([dt, jnp.exp(acs), dt * jnp.exp(last - acs)], axis=0)
    ph_hi = per_head.astype(BF16)
    ph_lo = (per_head - ph_hi.astype(F32)).astype(BF16)
    per_chan = _dot(ph_hi, e_ref[...]) + _dot(ph_lo, e_ref[...])
    dt_e, eacs_e, dtwj_e = per_chan[:L], per_chan[L:2 * L], per_chan[2 * L:]
    xdt_b = (xs * dt_e).astype(BF16)
    xw_b = (xs * dtwj_e).astype(BF16)
    e_last = jnp.broadcast_to(jnp.exp(acs_t[:, L - 1:L]), (SSM_HEADS, SSM_STATE))

    ii = lax.broadcasted_iota(I32, (L, L), 0)
    jj = lax.broadcasted_iota(I32, (L, L), 1)
    causal = ii >= jj
    first_head = lax.broadcasted_iota(I32, (L, 2 * SSM_HEAD_DIM), 1) < SSM_HEAD_DIM

    y_intra, y_state = [], []
    for g in range(SSM_GROUPS):
        b_g = xbc[:, SSM_INNER + g * SSM_STATE:SSM_INNER + (g + 1) * SSM_STATE].astype(BF16)
        c_off = SSM_INNER + SSM_GROUPS * SSM_STATE
        c_g = xbc[:, c_off + g * SSM_STATE:c_off + (g + 1) * SSM_STATE].astype(BF16)
        scores = _dot_nt(c_g, b_g)
        rows = slice(g * SSM_GROUP_WIDTH, (g + 1) * SSM_GROUP_WIDTH)
        s_g = s_scr[rows, :]
        y_state.append(_dot_nt(c_g, s_g.astype(BF16)))
        for pair in range(SSM_GROUP_WIDTH // (2 * SSM_HEAD_DIM)):
            h0 = g * (SSM_HEADS // SSM_GROUPS) + 2 * pair
            ms = []
            for h in (h0, h0 + 1):
                decay = jnp.where(causal, jnp.exp(acs[:, h:h + 1] - acs_t[h:h + 1, :]), 0.0)
                ms.append(scores * decay)
            m_pair = jnp.concatenate(ms, axis=1).astype(BF16)
            x_pair = xdt_b[:, h0 * SSM_HEAD_DIM:(h0 + 2) * SSM_HEAD_DIM]
            zero = jnp.zeros_like(x_pair)
            x_bd = jnp.concatenate([jnp.where(first_head, x_pair, zero), jnp.where(first_head, zero, x_pair)], axis=0)
            y_intra.append(_dot(m_pair, x_bd))
        heads = range(g * (SSM_HEADS // SSM_GROUPS), (g + 1) * (SSM_HEADS // SSM_GROUPS))
        row_scale = jnp.concatenate(
            [jnp.broadcast_to(e_last[h:h + 1, :], (SSM_HEAD_DIM, SSM_STATE)) for h in heads], axis=0)
        s_scr[rows, :] = s_g * row_scale + _dot_tn(xw_b[:, rows], b_g)

    y = jnp.concatenate(y_intra, axis=1) + jnp.concatenate(y_state, axis=1) * eacs_e + xs * dsk_ref[...]
    yz = y * zs_ref[...].astype(F32)
    normed = []
    for g in range(SSM_GROUPS):
        t = yz[:, g * SSM_GROUP_WIDTH:(g + 1) * SSM_GROUP_WIDTH]
        normed.append(t * lax.rsqrt(jnp.mean(t * t, axis=-1, keepdims=True) + EPS))
    yn_ref[...] = (jnp.concatenate(normed, axis=1) * nw_ref[...]).astype(BF16)

    @pl.when(c == pl.num_programs(1) - 1)
    def _():
        so_ref[0, 0] = s_scr[...]


def _alias_previous(prev, args, in_specs):
    aliases = {}
    for k, p in enumerate(prev):
        if p is not None:
            aliases[len(args)] = k
            args.append(p)
            in_specs.append(pl.BlockSpec(memory_space=pl.ANY))
    return aliases


def _ssd(zs, xbc, dt, a, cw, cb, dsk_e, nw, *, row0, nb, T, L, depth, layer, init, prev):
    n = zs.shape[0]
    nc = T // L
    blk0 = row0 // L
    row = lambda cols: pl.BlockSpec((L, cols), lambda b, c: (blk0 + b * nc + c, 0))
    tril = jnp.tril(jnp.ones((L, L), F32))
    expand = jnp.repeat(jnp.eye(SSM_HEADS, dtype=BF16), SSM_HEAD_DIM, axis=1)
    eye = jnp.eye(SSM_HEADS, dtype=F32)
    args = [zs, xbc, dt, a, cw, cb, dsk_e, nw, tril, expand, eye]
    in_specs = [row(SSM_INNER), row(SSM_CONV_DIM), row(SSM_HEADS), row(SSM_HEADS),
                _full(cw.shape), _full(cb.shape), _full(dsk_e.shape), _full(nw.shape),
                _full(tril.shape), _full(expand.shape), _full(eye.shape)]
    if init is not None:
        args += [init[0], init[1]]
        in_specs += [pl.BlockSpec((1, 1, SSM_CONV - 1, SSM_CONV_DIM), lambda b, c: (layer, b, 0, 0)),
                     pl.BlockSpec((1, 1, SSM_INNER, SSM_STATE), lambda b, c: (layer, b, 0, 0))]
    out_shape = [jax.ShapeDtypeStruct((n, SSM_INNER), BF16),
                 jax.ShapeDtypeStruct((depth, nb, SSM_CONV - 1, SSM_CONV_DIM), F32),
                 jax.ShapeDtypeStruct((depth, nb, SSM_INNER, SSM_STATE), F32)]
    aliases = _alias_previous(prev, args, in_specs)
    return pl.pallas_call(
        functools.partial(_ssd_kernel, L=L, has_init=init is not None),
        grid=(nb, nc),
        in_specs=in_specs,
        out_specs=[row(SSM_INNER),
                   pl.BlockSpec((1, 1, SSM_CONV - 1, SSM_CONV_DIM), lambda b, c: (layer, b, 0, 0)),
                   pl.BlockSpec((1, 1, SSM_INNER, SSM_STATE), lambda b, c: (layer, b, 0, 0))],
        out_shape=out_shape,
        scratch_shapes=[pltpu.VMEM((SSM_INNER, SSM_STATE), F32), pltpu.VMEM((8, SSM_CONV_DIM), F32)],
        input_output_aliases=aliases,
        compiler_params=_params("arbitrary", "arbitrary"),
        name=f"ssd_L{L}",
    )(*args)


def _hgrn_levels(C):
    s, out = C // 2, []
    while s >= HG_BAND:
        out.append(s)
        s //= 2
    return out


def _shift_in_block(x, d):
    if d == 0:
        return x
    n, w = x.shape
    return pltpu.roll(x.reshape(n // HG_BAND, HG_BAND, w), d, axis=1).reshape(n, w)


def _hgrn_kernel(q_ref, k_ref, v_ref, g_ref, og_ref, nw_ref, tril_ref, ones_ref, *rest, C, has_init):
    if has_init:
        s0_ref = rest[0]
    o_ref, so_ref, st_scr, pbuf, rbuf = rest[-5:]
    c = pl.program_id(1)
    K = HG_HEAD_DIM

    @pl.when(c == 0)
    def _():
        for h in range(HG_HEADS):
            st_scr[h] = s0_ref[0, 0, h].T if has_init else jnp.zeros((K, K), F32)

    q = q_ref[...].astype(F32)
    k = k_ref[...].astype(F32)
    v = v_ref[...].astype(F32)
    gcs = _dot_hi(tril_ref[...], g_ref[...])

    row_in_block = lax.broadcasted_iota(I32, (C, HG_DIM), 0) % HG_BAND
    for d in range(HG_BAND):
        ks = _shift_in_block(k, d)
        gs = _shift_in_block(gcs, d)
        e = jnp.where(row_in_block >= d, jnp.exp(gcs - gs), 0.0)
        p = (q * ks * e).astype(BF16)
        for h in range(HG_HEADS):
            r0 = (d * HG_HEADS + h) * C
            pbuf[r0:r0 + C, :] = p[:, h * K:(h + 1) * K]
    rbuf[...] = _dot(pbuf[...], ones_ref[...])

    ii = lax.broadcasted_iota(I32, (C, C), 0)
    jj = lax.broadcasted_iota(I32, (C, C), 1)
    levels = _hgrn_levels(C)
    att = [jnp.zeros((C, C), F32) for _ in range(HG_HEADS)]
    for s in levels:
        q_parts, k_parts = [], []
        zero = jnp.zeros((s, HG_DIM), F32)
        for lo in range(0, C, 2 * s):
            mid = lo + s
            ref_row = gcs[mid - 1:mid, :]
            q_parts += [zero, q[mid:mid + s] * jnp.exp(gcs[mid:mid + s] - ref_row)]
            k_parts += [k[lo:mid] * jnp.exp(ref_row - gcs[lo:mid]), zero]
        qt = jnp.concatenate(q_parts, axis=0).astype(BF16)
        kt = jnp.concatenate(k_parts, axis=0).astype(BF16)
        same_segment = (ii // (2 * s)) == (jj // (2 * s))
        for h in range(HG_HEADS):
            att[h] = att[h] + jnp.where(same_segment, _dot_nt(qt[:, h * K:(h + 1) * K], kt[:, h * K:(h + 1) * K]), 0.0)

    last = gcs[C - 1:C, :]
    q_state = (q * jnp.exp(gcs)).astype(BF16)
    k_last = (k * jnp.exp(last - gcs)).astype(BF16)
    e_last = jnp.exp(last)
    v_b = v_ref[...]
    outs = []
    for h in range(HG_HEADS):
        cols = slice(h * K, (h + 1) * K)
        st = st_scr[h]
        o = _dot_nt(q_state[:, cols], st.astype(BF16))
        if levels:
            o = o + _dot(att[h].astype(BF16), v_b[:, cols])
        for d in range(HG_BAND):
            r0 = (d * HG_HEADS + h) * C
            o = o + rbuf[r0:r0 + C, :] * _shift_in_block(v[:, cols], d)
        st_scr[h] = st * e_last[:, cols] + _dot_tn(v_b[:, cols], k_last[:, cols])
        on = o * lax.rsqrt(jnp.mean(o * o, axis=-1, keepdims=True) + EPS)
        outs.append(on)
    o_all = jnp.concatenate(outs, axis=1) * nw_ref[...] * og_ref[...].astype(F32)
    o_ref[...] = o_all.astype(BF16)

    @pl.when(c == pl.num_programs(1) - 1)
    def _():
        for h in range(HG_HEADS):
            so_ref[0, 0, h] = st_scr[h].T


def _hgrn(qs, kk, v, logf, ogs, nw_e, *, row0, nb, T, C, depth, layer, init, prev):
    n = qs.shape[0]
    nc = T // C
    blk0 = row0 // C
    row = pl.BlockSpec((C, HG_DIM), lambda b, c: (blk0 + b * nc + c, 0))
    K = HG_HEAD_DIM
    tril = jnp.tril(jnp.ones((C, C), F32))
    ones = jnp.ones((K, K), BF16)
    args = [qs, kk, v, logf, ogs, nw_e, tril, ones]
    in_specs = [row, row, row, row, row, _full(nw_e.shape), _full(tril.shape), _full(ones.shape)]
    state_spec = pl.BlockSpec((1, 1, HG_HEADS, K, K), lambda b, c: (layer, b, 0, 0, 0))
    if init is not None:
        args.append(init)
        in_specs.append(state_spec)
    out_shape = [jax.ShapeDtypeStruct((n, HG_DIM), BF16),
                 jax.ShapeDtypeStruct((depth, nb, HG_HEADS, K, K), F32)]
    aliases = _alias_previous(prev, args, in_specs)
    band_rows = HG_BAND * HG_HEADS * C
    return pl.pallas_call(
        functools.partial(_hgrn_kernel, C=C, has_init=init is not None),
        grid=(nb, nc),
        in_specs=in_specs,
        out_specs=[row, state_spec],
        out_shape=out_shape,
        scratch_shapes=[pltpu.VMEM((HG_HEADS, K, K), F32),
                        pltpu.VMEM((band_rows, K), BF16), pltpu.VMEM((band_rows, K), F32)],
        input_output_aliases=aliases,
        compiler_params=_params("arbitrary", "arbitrary"),
        name=f"hgrn_C{C}",
    )(*args)


def _merge_kernel(yn_ref, o_ref, x_ref, sc1_ref, sh1_ref, nw1_ref, wg_ref, g1_ref, sc2_ref, sh2_ref, nw2_ref,
                  wbs_ref, wbh_ref, wout_ref, rwt_ref, rb_ref, before_ref,
                  xo_ref, h2_ref, idx_ref, p_ref, rank_ref, cnt_ref, cnt_scr):
    hb = _mod_norm(x_ref[...], nw1_ref[...], sc1_ref[...], sh1_ref[...]).astype(BF16)
    gate_ssm = _sigmoid(_dot(hb, wg_ref[:, :D_MODEL]))
    gate_hgrn = _sigmoid(_dot(hb, wg_ref[:, D_MODEL:]))
    m = gate_ssm * _dot(yn_ref[...], wbs_ref[...]) + gate_hgrn * _dot(o_ref[...], wbh_ref[...])
    mo = _dot(m.astype(BF16), wout_ref[...])
    x3 = x_ref[...].reshape(ROW_TILE // MOD_ROWS, MOD_ROWS, D_MODEL)
    mo3 = mo.reshape(ROW_TILE // MOD_ROWS, MOD_ROWS, D_MODEL)
    xn = (x3 + g1_ref[...] * mo3).reshape(ROW_TILE, D_MODEL)
    xo_ref[...] = xn
    h2 = _mod_norm(xn, nw2_ref[...], sc2_ref[...], sh2_ref[...])
    _to_token_tiles(h2_ref, 0, h2)
    h2_hi = h2.astype(BF16)
    h2_lo = (h2 - h2_hi.astype(F32)).astype(BF16)
    logits = _dot_nt(rwt_ref[...], h2_hi) + _dot_nt(rwt_ref[...], h2_lo) + rb_ref[...]
    expert = lax.broadcasted_iota(I32, logits.shape, 0)
    vals, idxs, chosen = [], [], []
    for _ in range(TOP_K):
        mx = jnp.max(logits, axis=0, keepdims=True)
        first = jnp.min(jnp.where(logits == mx, expert, N_EXPERTS), axis=0, keepdims=True)
        sel = expert == first
        vals.append(mx)
        idxs.append(first)
        chosen.append(sel)
        logits = jnp.where(sel, -jnp.inf, logits)
    ev = [jnp.exp(v - vals[0]) for v in vals]
    den = ev[0] + ev[1] + ev[2] + ev[3]
    p_ref[...] = jnp.concatenate([e / den for e in ev] + [jnp.zeros((8 - TOP_K, ROW_TILE), F32)], axis=0)
    idx_ref[...] = jnp.concatenate(idxs, axis=0)

    @pl.when(pl.program_id(0) == 0)
    def _():
        cnt_scr[...] = jnp.zeros(cnt_scr.shape, F32)

    member = jnp.zeros(logits.shape, F32)
    for sel in chosen:
        member = member + sel.astype(F32)
    base = cnt_scr[...] + _dot(member.astype(BF16), before_ref[...])
    rank_ref[...] = jnp.concatenate(
        [jnp.sum(jnp.where(sel, base, 0.0), axis=0, keepdims=True) for sel in chosen], axis=0).astype(I32)
    cnt_scr[...] = cnt_scr[...] + jnp.sum(member, axis=1, keepdims=True)
    cnt_ref[...] = cnt_scr[:, :128]


def _merge(yn, o, x, sc1, sh1, nw1, wg, g1, sc2, sh2, nw2, wbs, wbh, wout, rwt, rb):
    n = x.shape[0]
    col_spec = lambda rows: pl.BlockSpec((rows, ROW_TILE), lambda i: (0, i))
    before = jnp.triu(jnp.ones((ROW_TILE, ROW_TILE), BF16), k=1)
    return pl.pallas_call(
        _merge_kernel,
        grid=(n // ROW_TILE,),
        in_specs=[_row_spec(SSM_INNER), _row_spec(HG_DIM), _row_spec(D_MODEL),
                  _mod_spec(), _mod_spec(), _full((1, D_MODEL)), _full(wg.shape),
                  _mod_spec(), _mod_spec(), _mod_spec(), _full((1, D_MODEL)),
                  _full(wbs.shape), _full(wbh.shape), _full(wout.shape), _full(rwt.shape), _full((N_EXPERTS, 1)),
                  _full(before.shape)],
        out_specs=[_row_spec(D_MODEL), pl.BlockSpec((ROW_TILE * TOKEN_TILE_ROWS, 128), lambda i: (i, 0)),
                   col_spec(TOP_K), col_spec(8), col_spec(TOP_K), _full((N_EXPERTS, 128))],
        out_shape=[jax.ShapeDtypeStruct((n, D_MODEL), F32), jax.ShapeDtypeStruct((n * TOKEN_TILE_ROWS, 128), F32),
                   jax.ShapeDtypeStruct((TOP_K, n), I32), jax.ShapeDtypeStruct((8, n), F32),
                   jax.ShapeDtypeStruct((TOP_K, n), I32), jax.ShapeDtypeStruct((N_EXPERTS, 128), F32)],
        scratch_shapes=[pltpu.VMEM((N_EXPERTS, ROW_TILE), F32)],
        compiler_params=_params("arbitrary"),
        name="merge_router",
    )(yn, o, x, sc1, sh1, nw1, wg, g1, sc2, sh2, nw2, wbs, wbh, wout, rwt, rb, before)


TOKEN_TILE_ROWS = D_MODEL // 128


def _to_token_tiles(ref, base, x):
    for c in range(TOKEN_TILE_ROWS):
        ref[pl.ds(base + c, x.shape[0], stride=TOKEN_TILE_ROWS), :] = x[:, c * 128:(c + 1) * 128]


def _from_token_tiles(ref, base, n):
    return jnp.concatenate([ref[pl.ds(base + c, n, stride=TOKEN_TILE_ROWS), :] for c in range(TOKEN_TILE_ROWS)],
                           axis=1)


def _row_copy(src, src_row, dst, dst_row, sem):
    def first(row):
        start = row * TOKEN_TILE_ROWS
        return start if isinstance(start, int) else pl.multiple_of(start, TOKEN_TILE_ROWS)
    return pltpu.make_async_copy(src.at[pl.ds(first(src_row), TOKEN_TILE_ROWS), :],
                                 dst.at[pl.ds(first(dst_row), TOKEN_TILE_ROWS), :], sem)


def _start_rows(idx_ref, copy_of_row, unrolled, priority=0):
    if unrolled:
        for r in range(EXPERT_TILE):
            copy_of_row(r, idx_ref[0, 0, r]).start(priority=priority)
    else:
        def body(r, carry):
            copy_of_row(r, idx_ref[0, 0, r]).start()
            return carry
        lax.fori_loop(0, EXPERT_TILE, body, 0, unroll=8)


def _gmm_kernel(te_ref, src_ref, src_next_ref, dst_ref, dst_prev_ref, h_hbm, wgu_ref, bgu_ref, wd_ref, bd_ref,
                y_hbm, xbuf, ybuf, gsem, ssem, wgu_b, wd_b):
    i = pl.program_id(0)
    last = pl.num_programs(0) - 1
    slot = i % 2
    half_rows = EXPERT_TILE * TOKEN_TILE_ROWS

    def half(buf_ref, buf):
        return buf_ref.at[pl.ds(pl.multiple_of(buf * half_rows, half_rows), half_rows), :]

    def gather(buf):
        return lambda r, row: _row_copy(h_hbm, row, half(xbuf, buf), r, gsem.at[buf])

    def scatter(buf):
        return lambda r, row: _row_copy(half(ybuf, buf), r, y_hbm, row, ssem.at[0])

    def wait_gather(buf):
        pltpu.make_async_copy(h_hbm.at[pl.ds(0, half_rows), :], half(xbuf, buf), gsem.at[buf]).wait()

    def wait_scatter(buf):
        pltpu.make_async_copy(half(ybuf, buf), y_hbm.at[pl.ds(0, half_rows), :], ssem.at[0]).wait()

    @pl.when(i == 0)
    def _():
        ybuf[...] = jnp.zeros(ybuf.shape, F32)
        spare = y_hbm.at[pl.ds(y_hbm.shape[0] - 2 * half_rows, 2 * half_rows), :]
        zero_spare = pltpu.make_async_copy(ybuf, spare, ssem.at[0])
        zero_spare.start()
        zero_spare.wait()
        _start_rows(src_ref, gather(0), unrolled=False)

    @pl.when(i > 0)
    def _():
        wait_scatter(slot)

    @pl.when(jnp.logical_or(i == 0, te_ref[i] != te_ref[jnp.maximum(i - 1, 0)]))
    def _():
        wgu_b[...] = wgu_ref[0].astype(BF16)
        wd_b[...] = wd_ref[0].astype(BF16)

    wait_gather(slot)
    base = pl.multiple_of(slot * half_rows, half_rows)
    x = _from_token_tiles(xbuf, base, EXPERT_TILE).astype(BF16)
    _start_rows(dst_prev_ref, scatter(1 - slot), unrolled=True, priority=0)
    _start_rows(src_next_ref, gather(1 - slot), unrolled=True, priority=1)
    gu = _dot(x, wgu_b[...]) + bgu_ref[0]
    gate = jnp.minimum(gu[:, :D_FF], SWIGLU_LIMIT)
    up = jnp.clip(gu[:, D_FF:], -SWIGLU_LIMIT, SWIGLU_LIMIT)
    act = gate * _sigmoid(SWIGLU_ALPHA * gate) * (up + 1.0)
    _to_token_tiles(ybuf, base, _dot(act.astype(BF16), wd_b[...]) + bd_ref[0])

    @pl.when(i == last)
    def _():
        wait_scatter(1 - slot)
        _start_rows(dst_ref, scatter(slot), unrolled=False)
        wait_scatter(slot)
        wait_gather(1 - slot)


def _gmm(tile_expert, src, dst_ext, h2, wgu, bgu, wd, bd, layer, n_out_rows):
    n_tiles = tile_expert.shape[0]
    smem_idx = lambda f: pl.BlockSpec((1, 1, EXPERT_TILE), f, memory_space=pltpu.SMEM)
    weight = lambda shape: pl.BlockSpec((1, 1) + shape, lambda i, te: (layer, te[i], 0, 0))
    grid_spec = pltpu.PrefetchScalarGridSpec(
        num_scalar_prefetch=1,
        grid=(n_tiles,),
        in_specs=[smem_idx(lambda i, te: (i, 0, 0)),
                  smem_idx(lambda i, te: (jnp.minimum(i + 1, n_tiles - 1), 0, 0)),
                  smem_idx(lambda i, te: (i + 1, 0, 0)),
                  smem_idx(lambda i, te: (i, 0, 0)),
                  pl.BlockSpec(memory_space=pl.ANY),
                  weight((D_MODEL, 2 * D_FF)), weight((1, 2 * D_FF)), weight((D_FF, D_MODEL)), weight((1, D_MODEL))],
        out_specs=pl.BlockSpec(memory_space=pl.ANY),
        scratch_shapes=[pltpu.VMEM((2 * EXPERT_TILE * TOKEN_TILE_ROWS, 128), F32),
                        pltpu.VMEM((2 * EXPERT_TILE * TOKEN_TILE_ROWS, 128), F32),
                        pltpu.SemaphoreType.DMA((2,)), pltpu.SemaphoreType.DMA((1,)),
                        pltpu.VMEM((D_MODEL, 2 * D_FF), BF16), pltpu.VMEM((D_FF, D_MODEL), BF16)],
    )
    return pl.pallas_call(
        _gmm_kernel_4d,
        grid_spec=grid_spec,
        out_shape=jax.ShapeDtypeStruct((n_out_rows * TOKEN_TILE_ROWS, 128), F32),
        compiler_params=_params("arbitrary"),
        name="moe_gmm",
    )(tile_expert, src, src, dst_ext, dst_ext, h2, wgu, bgu.reshape(bgu.shape[0], bgu.shape[1], 1, -1), wd,
      bd.reshape(bd.shape[0], bd.shape[1], 1, -1))


def _gmm_kernel_4d(te_ref, src_ref, src_next_ref, dst_ref, dst_prev_ref, h_hbm, wgu_ref, bgu_ref, wd_ref, bd_ref,
                   *rest):
    return _gmm_kernel(te_ref, src_ref, src_next_ref, dst_ref, dst_prev_ref, h_hbm, wgu_ref.at[0], bgu_ref.at[0],
                       wd_ref.at[0], bd_ref.at[0], *rest)


def _combine_kernel(y0_ref, y1_ref, y2_ref, y3_ref, p_ref, eye_ref, x_ref, g2_ref, fw_ref, xo_ref, *, final):
    p = _dot_nt(eye_ref[...], p_ref[...], precision=HI)
    acc = jnp.zeros((ROW_TILE, D_MODEL), F32)
    for k, y_ref in enumerate((y0_ref, y1_ref, y2_ref, y3_ref)):
        acc = acc + p[:, k:k + 1] * _from_token_tiles(y_ref, 0, ROW_TILE)
    x3 = x_ref[...].reshape(ROW_TILE // MOD_ROWS, MOD_ROWS, D_MODEL)
    acc3 = acc.reshape(ROW_TILE // MOD_ROWS, MOD_ROWS, D_MODEL)
    xn = (x3 + g2_ref[...] * acc3).reshape(ROW_TILE, D_MODEL)
    if final:
        xn = xn * lax.rsqrt(jnp.mean(xn * xn, axis=-1, keepdims=True) + EPS) * fw_ref[...]
    xo_ref[...] = xn


def _combine(y_pairs, p_t, x, g2, fw, final):
    n = x.shape[0]
    n_tiles = n // ROW_TILE
    eye = jnp.eye(ROW_TILE, dtype=F32)
    y_spec = lambda k: pl.BlockSpec((ROW_TILE * TOKEN_TILE_ROWS, 128), lambda i: (k * n_tiles + i, 0))
    return pl.pallas_call(
        functools.partial(_combine_kernel, final=final),
        grid=(n_tiles,),
        in_specs=[y_spec(0), y_spec(1), y_spec(2), y_spec(3),
                  pl.BlockSpec((8, ROW_TILE), lambda i: (0, i)), _full(eye.shape),
                  _row_spec(D_MODEL), _mod_spec(), _full((1, D_MODEL))],
        out_specs=_row_spec(D_MODEL),
        out_shape=jax.ShapeDtypeStruct((n, D_MODEL), F32),
        compiler_params=_params("arbitrary"),
        name="moe_combine",
    )(y_pairs, y_pairs, y_pairs, y_pairs, p_t, eye, x, g2, fw)


def _route(idx_t, rank_t, counts):
    n = idx_t.shape[1]
    n_pairs = TOP_K * n
    n_tiles = n_pairs // EXPERT_TILE + N_EXPERTS
    counts = counts[:, 0].astype(I32)
    padded = (counts + EXPERT_TILE - 1) // EXPERT_TILE * EXPERT_TILE
    ends = jnp.cumsum(padded)
    starts = ends - padded
    experts = jnp.arange(N_EXPERTS, dtype=I32)
    start_of = jnp.sum(jnp.where(idx_t[..., None] == experts, starts, 0), axis=-1)
    slots = (start_of + rank_t).reshape(-1)
    pair = jnp.full((n_tiles * EXPERT_TILE,), -1, I32).at[slots].set(jnp.arange(n_pairs, dtype=I32))
    row = jnp.arange(n_tiles * EXPERT_TILE, dtype=I32)
    spare = n_pairs + ((row // EXPERT_TILE) % 2) * EXPERT_TILE + row % EXPERT_TILE
    src = jnp.where(pair >= 0, pair % n, 0).reshape(n_tiles, 1, EXPERT_TILE)
    dst = jnp.where(pair >= 0, pair, spare)
    before_first = n_pairs + EXPERT_TILE + jnp.arange(EXPERT_TILE, dtype=I32)
    dst_ext = jnp.concatenate([before_first, dst]).reshape(n_tiles + 1, 1, EXPERT_TILE)
    tile_start = jnp.arange(n_tiles, dtype=I32) * EXPERT_TILE
    te = jnp.minimum(jnp.sum((ends[None, :] <= tile_start[:, None]).astype(I32), axis=1), N_EXPERTS - 1)
    last_used = jnp.max(jnp.where(counts > 0, experts, 0))
    te = jnp.where(tile_start < ends[-1], te, last_used)
    return src, dst_ext, te, n_pairs + 2 * EXPERT_TILE


def kernel(x_prompt, x_sample, c_prompt, c_sample, state_ssm, state_conv, state_hgrn, norm_mix_w, norm_ffn_w, ada_w, ada_b, w_in, conv_w, conv_b, dt_bias, a_log, d_skip, ssm_norm_w, hgrn_lb_logits, hgrn_norm_w, w_branch_ssm, w_branch_hgrn, w_out, router_w, router_b, w_gate_up, b_gate_up, w_down, b_down, final_norm_w):
    nbp, tp, _ = x_prompt.shape
    nbs, ts, _ = x_sample.shape
    depth = w_in.shape[0]
    n_p, n_s = nbp * tp, nbs * ts
    assert depth == 2 and tp % ROW_TILE == 0 and n_s % ROW_TILE == 0 and ts == MOD_ROWS
    l_p, l_s = math.gcd(tp, SSM_CHUNK), math.gcd(ts, SSM_CHUNK)
    c_p, c_s = math.gcd(tp, HG_CHUNK), math.gcd(ts, HG_CHUNK)
    assert c_s == HG_BAND and c_p % HG_BAND == 0

    x = jnp.concatenate([x_prompt.reshape(n_p, D_MODEL), x_sample.reshape(n_s, D_MODEL)], axis=0)
    c_all = jnp.concatenate([c_prompt, c_sample], axis=0)

    lb_all = jnp.cumsum(jax.nn.softmax(hgrn_lb_logits.astype(F32), axis=0), axis=0)
    lb_all = lb_all - lb_all[:1]

    state_ssm2 = state_ssm.reshape(depth, nbs, SSM_INNER, SSM_STATE)
    offs = [0, 2048, 5120, 5152, 6176, 7200, 8224, 9248, 10272, 11296]

    ssd_p = ssd_s = [None, None, None]
    hg_p = hg_s = [None, None]
    for l in range(depth):
        mod = _ada(c_all, ada_w[l], ada_b[l])
        mod_rows = jnp.concatenate([jnp.repeat(mod[:nbp], tp // MOD_ROWS, axis=0), mod[nbp:]], axis=0)
        sh1, sc1, g1, sh2, sc2, g2 = [m.reshape(-1, 1, D_MODEL) for m in jnp.split(mod_rows, 6, axis=-1)]

        wl = w_in[l]
        wz = wl[:, offs[0]:offs[1]].astype(BF16)
        wx = wl[:, offs[1]:offs[2]].astype(BF16)
        wdt = wl[:, offs[2]:offs[3]]
        whg = wl[:, offs[3]:offs[7]].astype(BF16)
        wgates = wl[:, offs[7]:].astype(BF16)
        nw1 = norm_mix_w[l].reshape(1, -1)

        zs, xbc, dt, a = _inproj_ssd(x, sc1, sh1, nw1, wz, wx, wdt, dt_bias[l].reshape(1, -1), a_log[l].reshape(1, -1))
        lb = lb_all[l].reshape(1, -1)
        qs, kk, logf, v, ogs = _inproj_hgrn(x, sc1, sh1, nw1, whg, jnp.log(lb), jnp.log1p(-lb), 1.0 - lb)

        ssd_args = (zs, xbc, dt, a, conv_w[l], conv_b[l].reshape(1, -1),
                    jnp.repeat(d_skip[l], SSM_HEAD_DIM).reshape(1, -1), ssm_norm_w[l].reshape(1, -1))
        ssd_p = _ssd(*ssd_args, row0=0, nb=nbp, T=tp, L=l_p, depth=depth, layer=l, init=None,
                     prev=[None] + list(ssd_p[1:]))
        ssd_s = _ssd(*ssd_args, row0=n_p, nb=nbs, T=ts, L=l_s, depth=depth, layer=l, init=(state_conv, state_ssm2),
                     prev=[ssd_p[0]] + list(ssd_s[1:]))
        yn = ssd_s[0]

        hg_args = (qs, kk, v, logf, ogs, jnp.tile(hgrn_norm_w[l], HG_HEADS).reshape(1, -1))
        hg_p = _hgrn(*hg_args, row0=0, nb=nbp, T=tp, C=c_p, depth=depth, layer=l, init=None,
                     prev=[None] + list(hg_p[1:]))
        hg_s = _hgrn(*hg_args, row0=n_p, nb=nbs, T=ts, C=c_s, depth=depth, layer=l, init=state_hgrn,
                     prev=[hg_p[0]] + list(hg_s[1:]))
        o = hg_s[0]

        x1, h2, idx_t, p_t, rank_t, counts = _merge(
            yn, o, x, sc1, sh1, nw1, wgates, g1, sc2, sh2, norm_ffn_w[l].reshape(1, -1),
            w_branch_ssm[l].astype(BF16), w_branch_hgrn[l].astype(BF16), w_out[l].astype(BF16),
            router_w[l].T.astype(BF16), router_b[l].reshape(-1, 1))
        src, dst_ext, te, n_out_rows = _route(idx_t, rank_t, counts)
        y_pairs = _gmm(te, src, dst_ext, h2, w_gate_up, b_gate_up, w_down, b_down, l, n_out_rows)
        x = _combine(y_pairs, p_t, x1, g2, final_norm_w.reshape(1, -1), final=(l == depth - 1))

    y_prompt = x[:n_p].reshape(nbp, tp, D_MODEL)
    y_sample = x[n_p:].reshape(nbs, ts, D_MODEL)
    ssm_p = ssd_p[2].reshape(depth, nbp, SSM_HEADS, SSM_HEAD_DIM, SSM_STATE)
    ssm_s = ssd_s[2].reshape(depth, nbs, SSM_HEADS, SSM_HEAD_DIM, SSM_STATE)
    return (y_prompt, y_sample, ssm_p, ssd_p[1], hg_p[1], ssm_s, ssd_s[1], hg_s[1])
```

```python
import functools
import math

import jax
import jax.numpy as jnp
from jax import lax
from jax.experimental import pallas as pl
from jax.experimental.pallas import tpu as pltpu

F32 = jnp.float32
BF16 = jnp.bfloat16
I32 = jnp.int32
HI = lax.Precision.HIGHEST

D_MODEL = 1024
SSM_INNER = 2048
SSM_HEAD_DIM = 64
SSM_HEADS = 32
SSM_GROUPS = 4
SSM_GROUP_WIDTH = SSM_INNER // SSM_GROUPS
SSM_STATE = 128
SSM_CONV = 4
SSM_CONV_DIM = 3072
SSM_CHUNK = 128
HG_DIM = 1024
HG_HEAD_DIM = 128
HG_HEADS = 8
HG_CHUNK = 64
HG_BAND = 8
N_EXPERTS = 32
TOP_K = 4
D_FF = 1024
SWIGLU_LIMIT = 7.0
SWIGLU_ALPHA = 1.702
EPS = 1e-6

ROW_TILE = 256
MOD_ROWS = 8
EXPERT_TILE = 256
VMEM_LIMIT_V7X = 56 * 1024 * 1024

_NT = (((1,), (1,)), ((), ()))
_TN = (((0,), (0,)), ((), ()))


def _params(*sem):
    return pltpu.CompilerParams(dimension_semantics=sem, vmem_limit_bytes=VMEM_LIMIT_V7X)


def _dot(a, b):
    return jnp.dot(a, b, preferred_element_type=F32)


def _dot_hi(a, b):
    return jnp.dot(a, b, precision=HI, preferred_element_type=F32)


def _dot_nt(a, b, precision=None):
    return lax.dot_general(a, b, _NT, precision=precision, preferred_element_type=F32)


def _dot_tn(a, b):
    return lax.dot_general(a, b, _TN, preferred_element_type=F32)


def _sigmoid(x):
    return 1.0 / (1.0 + jnp.exp(-x))


def _silu(x):
    return x * _sigmoid(x)


def _softplus(x):
    return jnp.maximum(x, 0.0) + jnp.log1p(jnp.exp(-jnp.abs(x)))


def _full(shape):
    nd = len(shape)
    return pl.BlockSpec(shape, lambda *_: (0,) * nd)


def _ada_kernel(c_ref, w_ref, b_ref, o_ref):
    o_ref[...] = _dot_hi(_silu(c_ref[...]), w_ref[...]) + b_ref[...]


def _ada(c_all, w, b):
    n = c_all.shape[0]
    tn = 1536
    return pl.pallas_call(
        _ada_kernel,
        grid=(6 * D_MODEL // tn,),
        in_specs=[_full((n, D_MODEL)),
                  pl.BlockSpec((D_MODEL, tn), lambda j: (0, j)),
                  pl.BlockSpec((1, tn), lambda j: (0, j))],
        out_specs=pl.BlockSpec((n, tn), lambda j: (0, j)),
        out_shape=jax.ShapeDtypeStruct((n, 6 * D_MODEL), F32),
        compiler_params=_params("arbitrary"),
        name="ada",
    )(c_all, w, b.reshape(1, -1))


def _mod_norm(x, nw, sc, sh):
    ms = jnp.mean(x * x, axis=-1, keepdims=True)
    y = x * lax.rsqrt(ms + EPS) * nw
    y3 = y.reshape(ROW_TILE // MOD_ROWS, MOD_ROWS, D_MODEL)
    return (y3 * (1.0 + sc) + sh).reshape(ROW_TILE, D_MODEL)


def _mod_spec():
    return pl.BlockSpec((ROW_TILE // MOD_ROWS, 1, D_MODEL), lambda i: (i, 0, 0))


def _row_spec(cols):
    return pl.BlockSpec((ROW_TILE, cols), lambda i: (i, 0))


def _inproj_ssd_kernel(x_ref, sc_ref, sh_ref, nw_ref, wz_ref, wx_ref, wdt_ref, dtb_ref, alog_ref,
                       zs_ref, xbc_ref, dt_ref, a_ref):
    h = _mod_norm(x_ref[...], nw_ref[...], sc_ref[...], sh_ref[...])
    hb = h.astype(BF16)
    zs_ref[...] = _silu(_dot(hb, wz_ref[...])).astype(BF16)
    xbc_ref[...] = _dot(hb, wx_ref[...])
    dt = _softplus(_dot_hi(h, wdt_ref[...]) + dtb_ref[...])
    dt_ref[...] = dt
    a_ref[...] = -jnp.exp(alog_ref[...]) * dt


def _inproj_ssd(x, sc, sh, nw, wz, wx, wdt, dtb, alog):
    n = x.shape[0]
    return pl.pallas_call(
        _inproj_ssd_kernel,
        grid=(n // ROW_TILE,),
        in_specs=[_row_spec(D_MODEL), _mod_spec(), _mod_spec(), _full((1, D_MODEL)),
                  _full(wz.shape), _full(wx.shape), _full(wdt.shape), _full((1, SSM_HEADS)), _full((1, SSM_HEADS))],
        out_specs=[_row_spec(SSM_INNER), _row_spec(SSM_CONV_DIM), _row_spec(SSM_HEADS), _row_spec(SSM_HEADS)],
        out_shape=[jax.ShapeDtypeStruct((n, SSM_INNER), BF16), jax.ShapeDtypeStruct((n, SSM_CONV_DIM), F32),
                   jax.ShapeDtypeStruct((n, SSM_HEADS), F32), jax.ShapeDtypeStruct((n, SSM_HEADS), F32)],
        compiler_params=_params("arbitrary"),
        name="inproj_ssd",
    )(x, sc, sh, nw, wz, wx, wdt, dtb, alog)


def _inproj_hgrn_kernel(x_ref, sc_ref, sh_ref, nw_ref, w_ref, loglb_ref, log1mlb_ref, oneml_ref,
                        qs_ref, kk_ref, logf_ref, v_ref, ogs_ref):
    h = _mod_norm(x_ref[...], nw_ref[...], sc_ref[...], sh_ref[...])
    hb = h.astype(BF16)

    def proj(k):
        return _dot(hb, w_ref[:, k * D_MODEL:(k + 1) * D_MODEL])

    qs_ref[...] = _silu(proj(0)).astype(BF16)
    fr = proj(1)
    log_sig = jnp.minimum(fr, 0.0) - jnp.log1p(jnp.exp(-jnp.abs(fr)))
    a = loglb_ref[...]
    b = log1mlb_ref[...] + log_sig
    logf_ref[...] = jnp.maximum(a, b) + jnp.log1p(jnp.exp(-jnp.abs(a - b)))
    kk_ref[...] = (oneml_ref[...] * _sigmoid(-fr)).astype(BF16)
    v_ref[...] = proj(2).astype(BF16)
    ogs_ref[...] = _silu(proj(3)).astype(BF16)


def _inproj_hgrn(x, sc, sh, nw, w, loglb, log1mlb, oneml):
    n = x.shape[0]
    vec = _full((1, HG_DIM))
    out = lambda dt: jax.ShapeDtypeStruct((n, HG_DIM), dt)
    return pl.pallas_call(
        _inproj_hgrn_kernel,
        grid=(n // ROW_TILE,),
        in_specs=[_row_spec(D_MODEL), _mod_spec(), _mod_spec(), vec, _full(w.shape), vec, vec, vec],
        out_specs=[_row_spec(HG_DIM)] * 5,
        out_shape=[out(BF16), out(BF16), out(F32), out(BF16), out(BF16)],
        compiler_params=_params("arbitrary"),
        name="inproj_hgrn",
    )(x, sc, sh, nw, w, loglb, log1mlb, oneml)


def _ssd_kernel(zs_ref, xbc_ref, dt_ref, a_ref, cw_ref, cb_ref, dsk_ref, nw_ref, tril_ref, e_ref, eye_ref,
                *rest, L, has_init):
    if has_init:
        cs0_ref, s0_ref = rest[:2]
    yn_ref, cso_ref, so_ref, s_scr, prev_scr = rest[-5:]
    c = pl.program_id(1)

    @pl.when(c == 0)
    def _():
        prev_scr[...] = jnp.zeros(prev_scr.shape, F32)
        if has_init:
            s_scr[...] = s0_ref[0, 0]
            prev_scr[8 - (SSM_CONV - 1):8, :] = cs0_ref[0, 0]
        else:
            s_scr[...] = jnp.zeros(s_scr.shape, F32)

    u = xbc_ref[...]
    ext = jnp.concatenate([prev_scr[...], u], axis=0).reshape(L // 8 + 1, 8, SSM_CONV_DIM)
    row_in_tile = lax.broadcasted_iota(I32, (L // 8, 8, SSM_CONV_DIM), 1)
    conv = cb_ref[...] + cw_ref[SSM_CONV - 1:SSM_CONV, :] * u
    for j in range(1, SSM_CONV):
        rot = pltpu.roll(ext, j, axis=1)
        shifted = jnp.where(row_in_tile < j, rot[:L // 8], rot[1:]).reshape(L, SSM_CONV_DIM)
        conv = conv + cw_ref[SSM_CONV - 1 - j:SSM_CONV - j, :] * shifted
    prev_scr[...] = u[L - 8:L, :]
    cso_ref[0, 0] = u[L - (SSM_CONV - 1):L, :]
    xbc = _silu(conv)
    xs = xbc[:, :SSM_INNER]

    a = a_ref[...]
    acs = _dot_hi(tril_ref[...], a)
    acs_t = _dot_nt(eye_ref[...], acs, precision=HI)
    last = acs[L - 1:L, :]
    dt = dt_ref[...]
    per_head = jnp.concatenate([dt, jnp.exp(acs), dt * jnp.exp(last - acs)], axis=0)
    ph_hi = per_head.astype(BF16)
    ph_lo = (per_head - ph_hi.astype(F32)).astype(BF16)
    per_chan = _dot(ph_hi, e_ref[...]) + _dot(ph_lo, e_ref[...])
    dt_e, eacs_e, dtwj_e = per_chan[:L], per_chan[L:2 * L], per_chan[2 * L:]
    xdt_b = (xs * dt_e).astype(BF16)
    xw_b = (xs * dtwj_e).astype(BF16)
    e_last = jnp.broadcast_to(jnp.exp(acs_t[:, L - 1:L]), (SSM_HEADS, SSM_STATE))

    ii = lax.broadcasted_iota(I32, (L, L), 0)
    jj = lax.broadcasted_iota(I32, (L, L), 1)
    causal = ii >= jj
    first_head = lax.broadcasted_iota(I32, (L, 2 * SSM_HEAD_DIM), 1) < SSM_HEAD_DIM

    y_intra, y_state = [], []
    for g in range(SSM_GROUPS):
        b_g = xbc[:, SSM_INNER + g * SSM_STATE:SSM_INNER + (g + 1) * SSM_STATE].astype(BF16)
        c_off = SSM_INNER + SSM_GROUPS * SSM_STATE
        c_g = xbc[:, c_off + g * SSM_STATE:c_off + (g + 1) * SSM_STATE].astype(BF16)
        scores = _dot_nt(c_g, b_g)
        rows = slice(g * SSM_GROUP_WIDTH, (g + 1) * SSM_GROUP_WIDTH)
        s_g = s_scr[rows, :]
        y_state.append(_dot_nt(c_g, s_g.astype(BF16)))
        for pair in range(SSM_GROUP_WIDTH // (2 * SSM_HEAD_DIM)):
            h0 = g * (SSM_HEADS // SSM_GROUPS) + 2 * pair
            ms = []
            for h in (h0, h0 + 1):
                decay = jnp.where(causal, jnp.exp(acs[:, h:h + 1] - acs_t[h:h + 1, :]), 0.0)
                ms.append(scores * decay)
            m_pair = jnp.concatenate(ms, axis=1).astype(BF16)
            x_pair = xdt_b[:, h0 * SSM_HEAD_DIM:(h0 + 2) * SSM_HEAD_DIM]
            zero = jnp.zeros_like(x_pair)
            x_bd = jnp.concatenate([jnp.where(first_head, x_pair, zero), jnp.where(first_head, zero, x_pair)], axis=0)
            y_intra.append(_dot(m_pair, x_bd))
        heads = range(g * (SSM_HEADS // SSM_GROUPS), (g + 1) * (SSM_HEADS // SSM_GROUPS))
        row_scale = jnp.concatenate(
            [jnp.broadcast_to(e_last[h:h + 1, :], (SSM_HEAD_DIM, SSM_STATE)) for h in heads], axis=0)
        s_scr[rows, :] = s_g * row_scale + _dot_tn(xw_b[:, rows], b_g)

    y = jnp.concatenate(y_intra, axis=1) + jnp.concatenate(y_state, axis=1) * eacs_e + xs * dsk_ref[...]
    yz = y * zs_ref[...].astype(F32)
    normed = []
    for g in range(SSM_GROUPS):
        t = yz[:, g * SSM_GROUP_WIDTH:(g + 1) * SSM_GROUP_WIDTH]
        normed.append(t * lax.rsqrt(jnp.mean(t * t, axis=-1, keepdims=True) + EPS))
    yn_ref[...] = (jnp.concatenate(normed, axis=1) * nw_ref[...]).astype(BF16)

    @pl.when(c == pl.num_programs(1) - 1)
    def _():
        so_ref[0, 0] = s_scr[...]


def _alias_previous(prev, args, in_specs):
    aliases = {}
    for k, p in enumerate(prev):
        if p is not None:
            aliases[len(args)] = k
            args.append(p)
            in_specs.append(pl.BlockSpec(memory_space=pl.ANY))
    return aliases


def _ssd(zs, xbc, dt, a, cw, cb, dsk_e, nw, *, row0, nb, T, L, depth, layer, init, prev):
    n = zs.shape[0]
    nc = T // L
    blk0 = row0 // L
    row = lambda cols: pl.BlockSpec((L, cols), lambda b, c: (blk0 + b * nc + c, 0))
    tril = jnp.tril(jnp.ones((L, L), F32))
    expand = jnp.repeat(jnp.eye(SSM_HEADS, dtype=BF16), SSM_HEAD_DIM, axis=1)
    eye = jnp.eye(SSM_HEADS, dtype=F32)
    args = [zs, xbc, dt, a, cw, cb, dsk_e, nw, tril, expand, eye]
    in_specs = [row(SSM_INNER), row(SSM_CONV_DIM), row(SSM_HEADS), row(SSM_HEADS),
                _full(cw.shape), _full(cb.shape), _full(dsk_e.shape), _full(nw.shape),
                _full(tril.shape), _full(expand.shape), _full(eye.shape)]
    if init is not None:
        args += [init[0], init[1]]
        in_specs += [pl.BlockSpec((1, 1, SSM_CONV - 1, SSM_CONV_DIM), lambda b, c: (layer, b, 0, 0)),
                     pl.BlockSpec((1, 1, SSM_INNER, SSM_STATE), lambda b, c: (layer, b, 0, 0))]
    out_shape = [jax.ShapeDtypeStruct((n, SSM_INNER), BF16),
                 jax.ShapeDtypeStruct((depth, nb, SSM_CONV - 1, SSM_CONV_DIM), F32),
                 jax.ShapeDtypeStruct((depth, nb, SSM_INNER, SSM_STATE), F32)]
    aliases = _alias_previous(prev, args, in_specs)
    return pl.pallas_call(
        functools.partial(_ssd_kernel, L=L, has_init=init is not None),
        grid=(nb, nc),
        in_specs=in_specs,
        out_specs=[row(SSM_INNER),
                   pl.BlockSpec((1, 1, SSM_CONV - 1, SSM_CONV_DIM), lambda b, c: (layer, b, 0, 0)),
                   pl.BlockSpec((1, 1, SSM_INNER, SSM_STATE), lambda b, c: (layer, b, 0, 0))],
        out_shape=out_shape,
        scratch_shapes=[pltpu.VMEM((SSM_INNER, SSM_STATE), F32), pltpu.VMEM((8, SSM_CONV_DIM), F32)],
        input_output_aliases=aliases,
        compiler_params=_params("arbitrary", "arbitrary"),
        name=f"ssd_L{L}",
    )(*args)


def _hgrn_levels(C):
    s, out = C // 2, []
    while s >= HG_BAND:
        out.append(s)
        s //= 2
    return out


def _shift_in_block(x, d):
    if d == 0:
        return x
    n, w = x.shape
    return pltpu.roll(x.reshape(n // HG_BAND, HG_BAND, w), d, axis=1).reshape(n, w)


def _hgrn_kernel(q_ref, k_ref, v_ref, g_ref, og_ref, nw_ref, tril_ref, ones_ref, *rest, C, has_init):
    if has_init:
        s0_ref = rest[0]
    o_ref, so_ref, st_scr, pbuf, rbuf = rest[-5:]
    c = pl.program_id(1)
    K = HG_HEAD_DIM

    @pl.when(c == 0)
    def _():
        for h in range(HG_HEADS):
            st_scr[h] = s0_ref[0, 0, h].T if has_init else jnp.zeros((K, K), F32)

    q = q_ref[...].astype(F32)
    k = k_ref[...].astype(F32)
    v = v_ref[...].astype(F32)
    gcs = _dot_hi(tril_ref[...], g_ref[...])

    row_in_block = lax.broadcasted_iota(I32, (C, HG_DIM), 0) % HG_BAND
    for d in range(HG_BAND):
        ks = _shift_in_block(k, d)
        gs = _shift_in_block(gcs, d)
        e = jnp.where(row_in_block >= d, jnp.exp(gcs - gs), 0.0)
        p = (q * ks * e).astype(BF16)
        for h in range(HG_HEADS):
            r0 = (d * HG_HEADS + h) * C
            pbuf[r0:r0 + C, :] = p[:, h * K:(h + 1) * K]
    rbuf[...] = _dot(pbuf[...], ones_ref[...])

    ii = lax.broadcasted_iota(I32, (C, C), 0)
    jj = lax.broadcasted_iota(I32, (C, C), 1)
    levels = _hgrn_levels(C)
    att = [jnp.zeros((C, C), F32) for _ in range(HG_HEADS)]
    for s in levels:
        q_parts, k_parts = [], []
        zero = jnp.zeros((s, HG_DIM), F32)
        for lo in range(0, C, 2 * s):
            mid = lo + s
            ref_row = gcs[mid - 1:mid, :]
            q_parts += [zero, q[mid:mid + s] * jnp.exp(gcs[mid:mid + s] - ref_row)]
            k_parts += [k[lo:mid] * jnp.exp(ref_row - gcs[lo:mid]), zero]
        qt = jnp.concatenate(q_parts, axis=0).astype(BF16)
        kt = jnp.concatenate(k_parts, axis=0).astype(BF16)
        same_segment = (ii // (2 * s)) == (jj // (2 * s))
        for h in range(HG_HEADS):
            att[h] = att[h] + jnp.where(same_segment, _dot_nt(qt[:, h * K:(h + 1) * K], kt[:, h * K:(h + 1) * K]), 0.0)

    last = gcs[C - 1:C, :]
    q_state = (q * jnp.exp(gcs)).astype(BF16)
    k_last = (k * jnp.exp(last - gcs)).astype(BF16)
    e_last = jnp.exp(last)
    v_b = v_ref[...]
    outs = []
    for h in range(HG_HEADS):
        cols = slice(h * K, (h + 1) * K)
        st = st_scr[h]
        o = _dot_nt(q_state[:, cols], st.astype(BF16))
        if levels:
            o = o + _dot(att[h].astype(BF16), v_b[:, cols])
        for d in range(HG_BAND):
            r0 = (d * HG_HEADS + h) * C
            o = o + rbuf[r0:r0 + C, :] * _shift_in_block(v[:, cols], d)
        st_scr[h] = st * e_last[:, cols] + _dot_tn(v_b[:, cols], k_last[:, cols])
        on = o * lax.rsqrt(jnp.mean(o * o, axis=-1, keepdims=True) + EPS)
        outs.append(on)
    o_all = jnp.concatenate(outs, axis=1) * nw_ref[...] * og_ref[...].astype(F32)
    o_ref[...] = o_all.astype(BF16)

    @pl.when(c == pl.num_programs(1) - 1)
    def _():
        for h in range(HG_HEADS):
            so_ref[0, 0, h] = st_scr[h].T


def _hgrn(qs, kk, v, logf, ogs, nw_e, *, row0, nb, T, C, depth, layer, init, prev):
    n = qs.shape[0]
    nc = T // C
    blk0 = row0 // C
    row = pl.BlockSpec((C, HG_DIM), lambda b, c: (blk0 + b * nc + c, 0))
    K = HG_HEAD_DIM
    tril = jnp.tril(jnp.ones((C, C), F32))
    ones = jnp.ones((K, K), BF16)
    args = [qs, kk, v, logf, ogs, nw_e, tril, ones]
    in_specs = [row, row, row, row, row, _full(nw_e.shape), _full(tril.shape), _full(ones.shape)]
    state_spec = pl.BlockSpec((1, 1, HG_HEADS, K, K), lambda b, c: (layer, b, 0, 0, 0))
    if init is not None:
        args.append(init)
        in_specs.append(state_spec)
    out_shape = [jax.ShapeDtypeStruct((n, HG_DIM), BF16),
                 jax.ShapeDtypeStruct((depth, nb, HG_HEADS, K, K), F32)]
    aliases = _alias_previous(prev, args, in_specs)
    band_rows = HG_BAND * HG_HEADS * C
    return pl.pallas_call(
        functools.partial(_hgrn_kernel, C=C, has_init=init is not None),
        grid=(nb, nc),
        in_specs=in_specs,
        out_specs=[row, state_spec],
        out_shape=out_shape,
        scratch_shapes=[pltpu.VMEM((HG_HEADS, K, K), F32),
                        pltpu.VMEM((band_rows, K), BF16), pltpu.VMEM((band_rows, K), F32)],
        input_output_aliases=aliases,
        compiler_params=_params("arbitrary", "arbitrary"),
        name=f"hgrn_C{C}",
    )(*args)


def _merge_kernel(yn_ref, o_ref, x_ref, sc1_ref, sh1_ref, nw1_ref, wg_ref, g1_ref, sc2_ref, sh2_ref, nw2_ref,
                  wbs_ref, wbh_ref, wout_ref, rwt_ref, rb_ref, before_ref,
                  xo_ref, h2_ref, idx_ref, p_ref, rank_ref, cnt_ref, cnt_scr):
    hb = _mod_norm(x_ref[...], nw1_ref[...], sc1_ref[...], sh1_ref[...]).astype(BF16)
    gate_ssm = _sigmoid(_dot(hb, wg_ref[:, :D_MODEL]))
    gate_hgrn = _sigmoid(_dot(hb, wg_ref[:, D_MODEL:]))
    m = gate_ssm * _dot(yn_ref[...], wbs_ref[...]) + gate_hgrn * _dot(o_ref[...], wbh_ref[...])
    mo = _dot(m.astype(BF16), wout_ref[...])
    x3 = x_ref[...].reshape(ROW_TILE // MOD_ROWS, MOD_ROWS, D_MODEL)
    mo3 = mo.reshape(ROW_TILE // MOD_ROWS, MOD_ROWS, D_MODEL)
    xn = (x3 + g1_ref[...] * mo3).reshape(ROW_TILE, D_MODEL)
    xo_ref[...] = xn
    h2 = _mod_norm(xn, nw2_ref[...], sc2_ref[...], sh2_ref[...])
    _to_token_tiles(h2_ref, 0, h2)
    h2_hi = h2.astype(BF16)
    h2_lo = (h2 - h2_hi.astype(F32)).astype(BF16)
    logits = _dot_nt(rwt_ref[...], h2_hi) + _dot_nt(rwt_ref[...], h2_lo) + rb_ref[...]
    expert = lax.broadcasted_iota(I32, logits.shape, 0)
    vals, idxs, chosen = [], [], []
    for _ in range(TOP_K):
        mx = jnp.max(logits, axis=0, keepdims=True)
        first = jnp.min(jnp.where(logits == mx, expert, N_EXPERTS), axis=0, keepdims=True)
        sel = expert == first
        vals.append(mx)
        idxs.append(first)
        chosen.append(sel)
        logits = jnp.where(sel, -jnp.inf, logits)
    ev = [jnp.exp(v - vals[0]) for v in vals]
    den = ev[0] + ev[1] + ev[2] + ev[3]
    p_ref[...] = jnp.concatenate([e / den for e in ev] + [jnp.zeros((8 - TOP_K, ROW_TILE), F32)], axis=0)
    idx_ref[...] = jnp.concatenate(idxs, axis=0)

    @pl.when(pl.program_id(0) == 0)
    def _():
        cnt_scr[...] = jnp.zeros(cnt_scr.shape, F32)

    member = jnp.zeros(logits.shape, F32)
    for sel in chosen:
        member = member + sel.astype(F32)
    base = cnt_scr[...] + _dot(member.astype(BF16), before_ref[...])
    rank_ref[...] = jnp.concatenate(
        [jnp.sum(jnp.where(sel, base, 0.0), axis=0, keepdims=True) for sel in chosen], axis=0).astype(I32)
    cnt_scr[...] = cnt_scr[...] + jnp.sum(member, axis=1, keepdims=True)
    cnt_ref[...] = cnt_scr[:, :128]


def _merge(yn, o, x, sc1, sh1, nw1, wg, g1, sc2, sh2, nw2, wbs, wbh, wout, rwt, rb):
    n = x.shape[0]
    col_spec = lambda rows: pl.BlockSpec((rows, ROW_TILE), lambda i: (0, i))
    before = jnp.triu(jnp.ones((ROW_TILE, ROW_TILE), BF16), k=1)
    return pl.pallas_call(
        _merge_kernel,
        grid=(n // ROW_TILE,),
        in_specs=[_row_spec(SSM_INNER), _row_spec(HG_DIM), _row_spec(D_MODEL),
                  _mod_spec(), _mod_spec(), _full((1, D_MODEL)), _full(wg.shape),
                  _mod_spec(), _mod_spec(), _mod_spec(), _full((1, D_MODEL)),
                  _full(wbs.shape), _full(wbh.shape), _full(wout.shape), _full(rwt.shape), _full((N_EXPERTS, 1)),
                  _full(before.shape)],
        out_specs=[_row_spec(D_MODEL), pl.BlockSpec((ROW_TILE * TOKEN_TILE_ROWS, 128), lambda i: (i, 0)),
                   col_spec(TOP_K), col_spec(8), col_spec(TOP_K), _full((N_EXPERTS, 128))],
        out_shape=[jax.ShapeDtypeStruct((n, D_MODEL), F32), jax.ShapeDtypeStruct((n * TOKEN_TILE_ROWS, 128), F32),
                   jax.ShapeDtypeStruct((TOP_K, n), I32), jax.ShapeDtypeStruct((8, n), F32),
                   jax.ShapeDtypeStruct((TOP_K, n), I32), jax.ShapeDtypeStruct((N_EXPERTS, 128), F32)],
        scratch_shapes=[pltpu.VMEM((N_EXPERTS, ROW_TILE), F32)],
        compiler_params=_params("arbitrary"),
        name="merge_router",
    )(yn, o, x, sc1, sh1, nw1, wg, g1, sc2, sh2, nw2, wbs, wbh, wout, rwt, rb, before)


TOKEN_TILE_ROWS = D_MODEL // 128


def _to_token_tiles(ref, base, x):
    for c in range(TOKEN_TILE_ROWS):
        ref[pl.ds(base + c, x.shape[0], stride=TOKEN_TILE_ROWS), :] = x[:, c * 128:(c + 1) * 128]


def _from_token_tiles(ref, base, n):
    return jnp.concatenate([ref[pl.ds(base + c, n, stride=TOKEN_TILE_ROWS), :] for c in range(TOKEN_TILE_ROWS)],
                           axis=1)


def _row_copy(src, src_row, dst, dst_row, sem):
    def first(row):
        start = row * TOKEN_TILE_ROWS
        return start if isinstance(start, int) else pl.multiple_of(start, TOKEN_TILE_ROWS)
    return pltpu.make_async_copy(src.at[pl.ds(first(src_row), TOKEN_TILE_ROWS), :],
                                 dst.at[pl.ds(first(dst_row), TOKEN_TILE_ROWS), :], sem)


def _start_rows(idx_ref, copy_of_row):
    def body(r, carry):
        copy_of_row(r, idx_ref[0, 0, r]).start()
        return carry
    lax.fori_loop(0, EXPERT_TILE, body, 0, unroll=8)


def _gmm_kernel(te_ref, src_ref, src_next_ref, dst_ref, dst_prev_ref, h_hbm, wgu_ref, bgu_ref, wd_ref, bd_ref,
                y_hbm, xbuf, ybuf, gsem, ssem, wgu_b, wd_b):
    i = pl.program_id(0)
    last = pl.num_programs(0) - 1
    slot = i % 2
    half_rows = EXPERT_TILE * TOKEN_TILE_ROWS

    def half(buf_ref, buf):
        return buf_ref.at[pl.ds(pl.multiple_of(buf * half_rows, half_rows), half_rows), :]

    def gather(buf):
        return lambda r, row: _row_copy(h_hbm, row, half(xbuf, buf), r, gsem.at[buf])

    def scatter(buf):
        return lambda r, row: _row_copy(half(ybuf, buf), r, y_hbm, row, ssem.at[0])

    def wait_gather(buf):
        pltpu.make_async_copy(h_hbm.at[pl.ds(0, half_rows), :], half(xbuf, buf), gsem.at[buf]).wait()

    def wait_scatter(buf):
        pltpu.make_async_copy(half(ybuf, buf), y_hbm.at[pl.ds(0, half_rows), :], ssem.at[0]).wait()

    @pl.when(i == 0)
    def _():
        ybuf[...] = jnp.zeros(ybuf.shape, F32)
        spare = y_hbm.at[pl.ds(y_hbm.shape[0] - 2 * half_rows, 2 * half_rows), :]
        zero_spare = pltpu.make_async_copy(ybuf, spare, ssem.at[0])
        zero_spare.start()
        zero_spare.wait()
        _start_rows(src_ref, gather(0))

    @pl.when(i > 0)
    def _():
        wait_scatter(slot)

    _start_rows(src_next_ref, gather(1 - slot))
    _start_rows(dst_prev_ref, scatter(1 - slot))

    @pl.when(jnp.logical_or(i == 0, te_ref[i] != te_ref[jnp.maximum(i - 1, 0)]))
    def _():
        wgu_b[...] = wgu_ref[0].astype(BF16)
        wd_b[...] = wd_ref[0].astype(BF16)

    wait_gather(slot)
    base = pl.multiple_of(slot * half_rows, half_rows)
    x = _from_token_tiles(xbuf, base, EXPERT_TILE).astype(BF16)
    gu = _dot(x, wgu_b[...]) + bgu_ref[0]
    gate = jnp.minimum(gu[:, :D_FF], SWIGLU_LIMIT)
    up = jnp.clip(gu[:, D_FF:], -SWIGLU_LIMIT, SWIGLU_LIMIT)
    act = gate * _sigmoid(SWIGLU_ALPHA * gate) * (up + 1.0)
    _to_token_tiles(ybuf, base, _dot(act.astype(BF16), wd_b[...]) + bd_ref[0])

    @pl.when(i == last)
    def _():
        wait_scatter(1 - slot)
        _start_rows(dst_ref, scatter(slot))
        wait_scatter(slot)
        wait_gather(1 - slot)


def _gmm(tile_expert, src, dst_ext, h2, wgu, bgu, wd, bd, layer, n_out_rows):
    n_tiles = tile_expert.shape[0]
    smem_idx = lambda f: pl.BlockSpec((1, 1, EXPERT_TILE), f, memory_space=pltpu.SMEM)
    weight = lambda shape: pl.BlockSpec((1, 1) + shape, lambda i, te: (layer, te[i], 0, 0))
    grid_spec = pltpu.PrefetchScalarGridSpec(
        num_scalar_prefetch=1,
        grid=(n_tiles,),
        in_specs=[smem_idx(lambda i, te: (i, 0, 0)),
                  smem_idx(lambda i, te: (jnp.minimum(i + 1, n_tiles - 1), 0, 0)),
                  smem_idx(lambda i, te: (i + 1, 0, 0)),
                  smem_idx(lambda i, te: (i, 0, 0)),
                  pl.BlockSpec(memory_space=pl.ANY),
                  weight((D_MODEL, 2 * D_FF)), weight((1, 2 * D_FF)), weight((D_FF, D_MODEL)), weight((1, D_MODEL))],
        out_specs=pl.BlockSpec(memory_space=pl.ANY),
        scratch_shapes=[pltpu.VMEM((2 * EXPERT_TILE * TOKEN_TILE_ROWS, 128), F32),
                        pltpu.VMEM((2 * EXPERT_TILE * TOKEN_TILE_ROWS, 128), F32),
                        pltpu.SemaphoreType.DMA((2,)), pltpu.SemaphoreType.DMA((1,)),
                        pltpu.VMEM((D_MODEL, 2 * D_FF), BF16), pltpu.VMEM((D_FF, D_MODEL), BF16)],
    )
    return pl.pallas_call(
        _gmm_kernel_4d,
        grid_spec=grid_spec,
        out_shape=jax.ShapeDtypeStruct((n_out_rows * TOKEN_TILE_ROWS, 128), F32),
        compiler_params=_params("arbitrary"),
        name="moe_gmm",
    )(tile_expert, src, src, dst_ext, dst_ext, h2, wgu, bgu.reshape(bgu.shape[0], bgu.shape[1], 1, -1), wd,
      bd.reshape(bd.shape[0], bd.shape[1], 1, -1))


def _gmm_kernel_4d(te_ref, src_ref, src_next_ref, dst_ref, dst_prev_ref, h_hbm, wgu_ref, bgu_ref, wd_ref, bd_ref,
                   *rest):
    return _gmm_kernel(te_ref, src_ref, src_next_ref, dst_ref, dst_prev_ref, h_hbm, wgu_ref.at[0], bgu_ref.at[0],
                       wd_ref.at[0], bd_ref.at[0], *rest)


def _combine_kernel(y0_ref, y1_ref, y2_ref, y3_ref, p_ref, eye_ref, x_ref, g2_ref, fw_ref, xo_ref, *, final):
    p = _dot_nt(eye_ref[...], p_ref[...], precision=HI)
    acc = jnp.zeros((ROW_TILE, D_MODEL), F32)
    for k, y_ref in enumerate((y0_ref, y1_ref, y2_ref, y3_ref)):
        acc = acc + p[:, k:k + 1] * _from_token_tiles(y_ref, 0, ROW_TILE)
    x3 = x_ref[...].reshape(ROW_TILE // MOD_ROWS, MOD_ROWS, D_MODEL)
    acc3 = acc.reshape(ROW_TILE // MOD_ROWS, MOD_ROWS, D_MODEL)
    xn = (x3 + g2_ref[...] * acc3).reshape(ROW_TILE, D_MODEL)
    if final:
        xn = xn * lax.rsqrt(jnp.mean(xn * xn, axis=-1, keepdims=True) + EPS) * fw_ref[...]
    xo_ref[...] = xn


def _combine(y_pairs, p_t, x, g2, fw, final):
    n = x.shape[0]
    n_tiles = n // ROW_TILE
    eye = jnp.eye(ROW_TILE, dtype=F32)
    y_spec = lambda k: pl.BlockSpec((ROW_TILE * TOKEN_TILE_ROWS, 128), lambda i: (k * n_tiles + i, 0))
    return pl.pallas_call(
        functools.partial(_combine_kernel, final=final),
        grid=(n_tiles,),
        in_specs=[y_spec(0), y_spec(1), y_spec(2), y_spec(3),
                  pl.BlockSpec((8, ROW_TILE), lambda i: (0, i)), _full(eye.shape),
                  _row_spec(D_MODEL), _mod_spec(), _full((1, D_MODEL))],
        out_specs=_row_spec(D_MODEL),
        out_shape=jax.ShapeDtypeStruct((n, D_MODEL), F32),
        compiler_params=_params("arbitrary"),
        name="moe_combine",
    )(y_pairs, y_pairs, y_pairs, y_pairs, p_t, eye, x, g2, fw)


def _route(idx_t, rank_t, counts):
    n = idx_t.shape[1]
    n_pairs = TOP_K * n
    n_tiles = n_pairs // EXPERT_TILE + N_EXPERTS
    counts = counts[:, 0].astype(I32)
    padded = (counts + EXPERT_TILE - 1) // EXPERT_TILE * EXPERT_TILE
    ends = jnp.cumsum(padded)
    starts = ends - padded
    experts = jnp.arange(N_EXPERTS, dtype=I32)
    start_of = jnp.sum(jnp.where(idx_t[..., None] == experts, starts, 0), axis=-1)
    slots = (start_of + rank_t).reshape(-1)
    pair = jnp.full((n_tiles * EXPERT_TILE,), -1, I32).at[slots].set(jnp.arange(n_pairs, dtype=I32))
    row = jnp.arange(n_tiles * EXPERT_TILE, dtype=I32)
    spare = n_pairs + ((row // EXPERT_TILE) % 2) * EXPERT_TILE + row % EXPERT_TILE
    src = jnp.where(pair >= 0, pair % n, 0).reshape(n_tiles, 1, EXPERT_TILE)
    dst = jnp.where(pair >= 0, pair, spare)
    before_first = n_pairs + EXPERT_TILE + jnp.arange(EXPERT_TILE, dtype=I32)
    dst_ext = jnp.concatenate([before_first, dst]).reshape(n_tiles + 1, 1, EXPERT_TILE)
    tile_start = jnp.arange(n_tiles, dtype=I32) * EXPERT_TILE
    te = jnp.minimum(jnp.sum((ends[None, :] <= tile_start[:, None]).astype(I32), axis=1), N_EXPERTS - 1)
    last_used = jnp.max(jnp.where(counts > 0, experts, 0))
    te = jnp.where(tile_start < ends[-1], te, last_used)
    return src, dst_ext, te, n_pairs + 2 * EXPERT_TILE


def kernel(x_prompt, x_sample, c_prompt, c_sample, state_ssm, state_conv, state_hgrn, norm_mix_w, norm_ffn_w, ada_w, ada_b, w_in, conv_w, conv_b, dt_bias, a_log, d_skip, ssm_norm_w, hgrn_lb_logits, hgrn_norm_w, w_branch_ssm, w_branch_hgrn, w_out, router_w, router_b, w_gate_up, b_gate_up, w_down, b_down, final_norm_w):
    nbp, tp, _ = x_prompt.shape
    nbs, ts, _ = x_sample.shape
    depth = w_in.shape[0]
    n_p, n_s = nbp * tp, nbs * ts
    assert depth == 2 and tp % ROW_TILE == 0 and n_s % ROW_TILE == 0 and ts == MOD_ROWS
    l_p, l_s = math.gcd(tp, SSM_CHUNK), math.gcd(ts, SSM_CHUNK)
    c_p, c_s = math.gcd(tp, HG_CHUNK), math.gcd(ts, HG_CHUNK)
    assert c_s == HG_BAND and c_p % HG_BAND == 0

    x = jnp.concatenate([x_prompt.reshape(n_p, D_MODEL), x_sample.reshape(n_s, D_MODEL)], axis=0)
    c_all = jnp.concatenate([c_prompt, c_sample], axis=0)

    lb_all = jnp.cumsum(jax.nn.softmax(hgrn_lb_logits.astype(F32), axis=0), axis=0)
    lb_all = lb_all - lb_all[:1]

    state_ssm2 = state_ssm.reshape(depth, nbs, SSM_INNER, SSM_STATE)
    offs = [0, 2048, 5120, 5152, 6176, 7200, 8224, 9248, 10272, 11296]

    ssd_p = ssd_s = [None, None, None]
    hg_p = hg_s = [None, None]
    for l in range(depth):
        mod = _ada(c_all, ada_w[l], ada_b[l])
        mod_rows = jnp.concatenate([jnp.repeat(mod[:nbp], tp // MOD_ROWS, axis=0), mod[nbp:]], axis=0)
        sh1, sc1, g1, sh2, sc2, g2 = [m.reshape(-1, 1, D_MODEL) for m in jnp.split(mod_rows, 6, axis=-1)]

        wl = w_in[l]
        wz = wl[:, offs[0]:offs[1]].astype(BF16)
        wx = wl[:, offs[1]:offs[2]].astype(BF16)
        wdt = wl[:, offs[2]:offs[3]]
        whg = wl[:, offs[3]:offs[7]].astype(BF16)
        wgates = wl[:, offs[7]:].astype(BF16)
        nw1 = norm_mix_w[l].reshape(1, -1)

        zs, xbc, dt, a = _inproj_ssd(x, sc1, sh1, nw1, wz, wx, wdt, dt_bias[l].reshape(1, -1), a_log[l].reshape(1, -1))
        lb = lb_all[l].reshape(1, -1)
        qs, kk, logf, v, ogs = _inproj_hgrn(x, sc1, sh1, nw1, whg, jnp.log(lb), jnp.log1p(-lb), 1.0 - lb)

        ssd_args = (zs, xbc, dt, a, conv_w[l], conv_b[l].reshape(1, -1),
                    jnp.repeat(d_skip[l], SSM_HEAD_DIM).reshape(1, -1), ssm_norm_w[l].reshape(1, -1))
        ssd_p = _ssd(*ssd_args, row0=0, nb=nbp, T=tp, L=l_p, depth=depth, layer=l, init=None,
                     prev=[None] + list(ssd_p[1:]))
        ssd_s = _ssd(*ssd_args, row0=n_p, nb=nbs, T=ts, L=l_s, depth=depth, layer=l, init=(state_conv, state_ssm2),
                     prev=[ssd_p[0]] + list(ssd_s[1:]))
        yn = ssd_s[0]

        hg_args = (qs, kk, v, logf, ogs, jnp.tile(hgrn_norm_w[l], HG_HEADS).reshape(1, -1))
        hg_p = _hgrn(*hg_args, row0=0, nb=nbp, T=tp, C=c_p, depth=depth, layer=l, init=None,
                     prev=[None] + list(hg_p[1:]))
        hg_s = _hgrn(*hg_args, row0=n_p, nb=nbs, T=ts, C=c_s, depth=depth, layer=l, init=state_hgrn,
                     prev=[hg_p[0]] + list(hg_s[1:]))
        o = hg_s[0]

        x1, h2, idx_t, p_t, rank_t, counts = _merge(
            yn, o, x, sc1, sh1, nw1, wgates, g1, sc2, sh2, norm_ffn_w[l].reshape(1, -1),
            w_branch_ssm[l].astype(BF16), w_branch_hgrn[l].astype(BF16), w_out[l].astype(BF16),
            router_w[l].T.astype(BF16), router_b[l].reshape(-1, 1))
        src, dst_ext, te, n_out_rows = _route(idx_t, rank_t, counts)
        y_pairs = _gmm(te, src, dst_ext, h2, w_gate_up, b_gate_up, w_down, b_down, l, n_out_rows)
        x = _combine(y_pairs, p_t, x1, g2, final_norm_w.reshape(1, -1), final=(l == depth - 1))

    y_prompt = x[:n_p].reshape(nbp, tp, D_MODEL)
    y_sample = x[n_p:].reshape(nbs, ts, D_MODEL)
    ssm_p = ssd_p[2].reshape(depth, nbp, SSM_HEADS, SSM_HEAD_DIM, SSM_STATE)
    ssm_s = ssd_s[2].reshape(depth, nbs, SSM_HEADS, SSM_HEAD_DIM, SSM_STATE)
    return (y_prompt, y_sample, ssm_p, ssd_p[1], hg_p[1], ssm_s, ssd_s[1], hg_s[1])
```

```python
import functools
import math

import jax
import jax.numpy as jnp
from jax import lax
from jax.experimental import pallas as pl
from jax.experimental.pallas import tpu as pltpu

F32 = jnp.float32
BF16 = jnp.bfloat16
I32 = jnp.int32
HI = lax.Precision.HIGHEST

D_MODEL = 1024
SSM_INNER = 2048
SSM_HEAD_DIM = 64
SSM_HEADS = 32
SSM_GROUPS = 4
SSM_GROUP_WIDTH = SSM_INNER // SSM_GROUPS
SSM_STATE = 128
SSM_CONV = 4
SSM_CONV_DIM = 3072
SSM_CHUNK = 128
HG_DIM = 1024
HG_HEAD_DIM = 128
HG_HEADS = 8
HG_CHUNK = 64
HG_BAND = 8
N_EXPERTS = 32
TOP_K = 4
D_FF = 1024
SWIGLU_LIMIT = 7.0
SWIGLU_ALPHA = 1.702
EPS = 1e-6

ROW_TILE = 256
MOD_ROWS = 8
EXPERT_TILE = 256
VMEM_LIMIT_V7X = 56 * 1024 * 1024

_NT = (((1,), (1,)), ((), ()))
_TN = (((0,), (0,)), ((), ()))


def _params(*sem):
    return pltpu.CompilerParams(dimension_semantics=sem, vmem_limit_bytes=VMEM_LIMIT_V7X)


def _dot(a, b):
    return jnp.dot(a, b, preferred_element_type=F32)


def _dot_hi(a, b):
    return jnp.dot(a, b, precision=HI, preferred_element_type=F32)


def _dot_nt(a, b, precision=None):
    return lax.dot_general(a, b, _NT, precision=precision, preferred_element_type=F32)


def _dot_tn(a, b):
    return lax.dot_general(a, b, _TN, preferred_element_type=F32)


def _sigmoid(x):
    return 1.0 / (1.0 + jnp.exp(-x))


def _silu(x):
    return x * _sigmoid(x)


def _softplus(x):
    return jnp.maximum(x, 0.0) + jnp.log1p(jnp.exp(-jnp.abs(x)))


def _full(shape):
    nd = len(shape)
    return pl.BlockSpec(shape, lambda *_: (0,) * nd)


def _ada_kernel(c_ref, w_ref, b_ref, o_ref):
    o_ref[...] = _dot_hi(_silu(c_ref[...]), w_ref[...]) + b_ref[...]


def _ada(c_all, w, b):
    n = c_all.shape[0]
    tn = 1536
    return pl.pallas_call(
        _ada_kernel,
        grid=(6 * D_MODEL // tn,),
        in_specs=[_full((n, D_MODEL)),
                  pl.BlockSpec((D_MODEL, tn), lambda j: (0, j)),
                  pl.BlockSpec((1, tn), lambda j: (0, j))],
        out_specs=pl.BlockSpec((n, tn), lambda j: (0, j)),
        out_shape=jax.ShapeDtypeStruct((n, 6 * D_MODEL), F32),
        compiler_params=_params("arbitrary"),
        name="ada",
    )(c_all, w, b.reshape(1, -1))


def _mod_norm(x, nw, sc, sh):
    ms = jnp.mean(x * x, axis=-1, keepdims=True)
    y = x * lax.rsqrt(ms + EPS) * nw
    y3 = y.reshape(ROW_TILE // MOD_ROWS, MOD_ROWS, D_MODEL)
    return (y3 * (1.0 + sc) + sh).reshape(ROW_TILE, D_MODEL)


def _mod_spec():
    return pl.BlockSpec((ROW_TILE // MOD_ROWS, 1, D_MODEL), lambda i: (i, 0, 0))


def _row_spec(cols):
    return pl.BlockSpec((ROW_TILE, cols), lambda i: (i, 0))


def _inproj_ssd_kernel(x_ref, sc_ref, sh_ref, nw_ref, wz_ref, wx_ref, wdt_ref, dtb_ref, alog_ref,
                       zs_ref, xbc_ref, dt_ref, a_ref):
    h = _mod_norm(x_ref[...], nw_ref[...], sc_ref[...], sh_ref[...])
    hb = h.astype(BF16)
    zs_ref[...] = _silu(_dot(hb, wz_ref[...])).astype(BF16)
    xbc_ref[...] = _dot(hb, wx_ref[...])
    dt = _softplus(_dot_hi(h, wdt_ref[...]) + dtb_ref[...])
    dt_ref[...] = dt
    a_ref[...] = -jnp.exp(alog_ref[...]) * dt


def _inproj_ssd(x, sc, sh, nw, wz, wx, wdt, dtb, alog):
    n = x.shape[0]
    return pl.pallas_call(
        _inproj_ssd_kernel,
        grid=(n // ROW_TILE,),
        in_specs=[_row_spec(D_MODEL), _mod_spec(), _mod_spec(), _full((1, D_MODEL)),
                  _full(wz.shape), _full(wx.shape), _full(wdt.shape), _full((1, SSM_HEADS)), _full((1, SSM_HEADS))],
        out_specs=[_row_spec(SSM_INNER), _row_spec(SSM_CONV_DIM), _row_spec(SSM_HEADS), _row_spec(SSM_HEADS)],
        out_shape=[jax.ShapeDtypeStruct((n, SSM_INNER), BF16), jax.ShapeDtypeStruct((n, SSM_CONV_DIM), F32),
                   jax.ShapeDtypeStruct((n, SSM_HEADS), F32), jax.ShapeDtypeStruct((n, SSM_HEADS), F32)],
        compiler_params=_params("arbitrary"),
        name="inproj_ssd",
    )(x, sc, sh, nw, wz, wx, wdt, dtb, alog)


def _inproj_hgrn_kernel(x_ref, sc_ref, sh_ref, nw_ref, w_ref, loglb_ref, log1mlb_ref, oneml_ref,
                        qs_ref, kk_ref, logf_ref, v_ref, ogs_ref):
    h = _mod_norm(x_ref[...], nw_ref[...], sc_ref[...], sh_ref[...])
    hb = h.astype(BF16)

    def proj(k):
        return _dot(hb, w_ref[:, k * D_MODEL:(k + 1) * D_MODEL])

    qs_ref[...] = _silu(proj(0)).astype(BF16)
    fr = proj(1)
    log_sig = jnp.minimum(fr, 0.0) - jnp.log1p(jnp.exp(-jnp.abs(fr)))
    a = loglb_ref[...]
    b = log1mlb_ref[...] + log_sig
    logf_ref[...] = jnp.maximum(a, b) + jnp.log1p(jnp.exp(-jnp.abs(a - b)))
    kk_ref[...] = (oneml_ref[...] * _sigmoid(-fr)).astype(BF16)
    v_ref[...] = proj(2).astype(BF16)
    ogs_ref[...] = _silu(proj(3)).astype(BF16)


def _inproj_hgrn(x, sc, sh, nw, w, loglb, log1mlb, oneml):
    n = x.shape[0]
    vec = _full((1, HG_DIM))
    out = lambda dt: jax.ShapeDtypeStruct((n, HG_DIM), dt)
    return pl.pallas_call(
        _inproj_hgrn_kernel,
        grid=(n // ROW_TILE,),
        in_specs=[_row_spec(D_MODEL), _mod_spec(), _mod_spec(), vec, _full(w.shape), vec, vec, vec],
        out_specs=[_row_spec(HG_DIM)] * 5,
        out_shape=[out(BF16), out(BF16), out(F32), out(BF16), out(BF16)],
        compiler_params=_params("arbitrary"),
        name="inproj_hgrn",
    )(x, sc, sh, nw, w, loglb, log1mlb, oneml)


def _inproj_kernel(x_ref, sc_ref, sh_ref, nw_ref, wz_ref, wx_ref, wdt_ref, dtb_ref, alog_ref,
                   w_ref, loglb_ref, log1mlb_ref, oneml_ref,
                   zs_ref, xbc_ref, dt_ref, a_ref, qs_ref, kk_ref, logf_ref, v_ref, ogs_ref):
    h = _mod_norm(x_ref[...], nw_ref[...], sc_ref[...], sh_ref[...])
    hb = h.astype(BF16)
    zs_ref[...] = _silu(_dot(hb, wz_ref[...])).astype(BF16)
    xbc_ref[...] = _dot(hb, wx_ref[...])
    dt = _softplus(_dot_hi(h, wdt_ref[...]) + dtb_ref[...])
    dt_ref[...] = dt
    a_ref[...] = -jnp.exp(alog_ref[...]) * dt

    def proj(k):
        return _dot(hb, w_ref[:, k * D_MODEL:(k + 1) * D_MODEL])

    qs_ref[...] = _silu(proj(0)).astype(BF16)
    fr = proj(1)
    log_sig = jnp.minimum(fr, 0.0) - jnp.log1p(jnp.exp(-jnp.abs(fr)))
    a = loglb_ref[...]
    b = log1mlb_ref[...] + log_sig
    logf_ref[...] = jnp.maximum(a, b) + jnp.log1p(jnp.exp(-jnp.abs(a - b)))
    kk_ref[...] = (oneml_ref[...] * _sigmoid(-fr)).astype(BF16)
    v_ref[...] = proj(2).astype(BF16)
    ogs_ref[...] = _silu(proj(3)).astype(BF16)


def _inproj(x, sc, sh, nw, wz, wx, wdt, dtb, alog, w, loglb, log1mlb, oneml):
    n = x.shape[0]
    vec = _full((1, HG_DIM))
    hg = lambda dt: jax.ShapeDtypeStruct((n, HG_DIM), dt)
    return pl.pallas_call(
        _inproj_kernel,
        grid=(n // ROW_TILE,),
        in_specs=[_row_spec(D_MODEL), _mod_spec(), _mod_spec(), _full((1, D_MODEL)),
                  _full(wz.shape), _full(wx.shape), _full(wdt.shape), _full((1, SSM_HEADS)), _full((1, SSM_HEADS)),
                  _full(w.shape), vec, vec, vec],
        out_specs=[_row_spec(SSM_INNER), _row_spec(SSM_CONV_DIM), _row_spec(SSM_HEADS), _row_spec(SSM_HEADS)]
        + [_row_spec(HG_DIM)] * 5,
        out_shape=[jax.ShapeDtypeStruct((n, SSM_INNER), BF16), jax.ShapeDtypeStruct((n, SSM_CONV_DIM), F32),
                   jax.ShapeDtypeStruct((n, SSM_HEADS), F32), jax.ShapeDtypeStruct((n, SSM_HEADS), F32),
                   hg(BF16), hg(BF16), hg(F32), hg(BF16), hg(BF16)],
        compiler_params=_params("arbitrary"),
        name="inproj",
    )(x, sc, sh, nw, wz, wx, wdt, dtb, alog, w, loglb, log1mlb, oneml)


def _ssd_kernel(zs_ref, xbc_ref, dt_ref, a_ref, cw_ref, cb_ref, dsk_ref, nw_ref, tril_ref, e_ref, eye_ref,
                *rest, L, has_init):
    if has_init:
        cs0_ref, s0_ref = rest[:2]
    yn_ref, cso_ref, so_ref, s_scr, prev_scr = rest[-5:]
    c = pl.program_id(1)

    @pl.when(c == 0)
    def _():
        prev_scr[...] = jnp.zeros(prev_scr.shape, F32)
        if has_init:
            s_scr[...] = s0_ref[0, 0]
            prev_scr[8 - (SSM_CONV - 1):8, :] = cs0_ref[0, 0]
        else:
            s_scr[...] = jnp.zeros(s_scr.shape, F32)

    u = xbc_ref[...]
    ext = jnp.concatenate([prev_scr[...], u], axis=0).reshape(L // 8 + 1, 8, SSM_CONV_DIM)
    row_in_tile = lax.broadcasted_iota(I32, (L // 8, 8, SSM_CONV_DIM), 1)
    conv = cb_ref[...] + cw_ref[SSM_CONV - 1:SSM_CONV, :] * u
    for j in range(1, SSM_CONV):
        rot = pltpu.roll(ext, j, axis=1)
        shifted = jnp.where(row_in_tile < j, rot[:L // 8], rot[1:]).reshape(L, SSM_CONV_DIM)
        conv = conv + cw_ref[SSM_CONV - 1 - j:SSM_CONV - j, :] * shifted
    prev_scr[...] = u[L - 8:L, :]
    cso_ref[0, 0] = u[L - (SSM_CONV - 1):L, :]
    xbc = _silu(conv)
    xs = xbc[:, :SSM_INNER]

    a = a_ref[...]
    acs = _dot_hi(tril_ref[...], a)
    acs_t = _dot_nt(eye_ref[...], acs, precision=HI)
    last = acs[L - 1:L, :]
    dt = dt_ref[...]
    per_head = jnp.concatenate([dt, jnp.exp(acs), dt * jnp.exp(last - acs)], axis=0)
    ph_hi = per_head.astype(BF16)
    ph_lo = (per_head - ph_hi.astype(F32)).astype(BF16)
    per_chan = _dot(ph_hi, e_ref[...]) + _dot(ph_lo, e_ref[...])
    dt_e, eacs_e, dtwj_e = per_chan[:L], per_chan[L:2 * L], per_chan[2 * L:]
    xdt_b = (xs * dt_e).astype(BF16)
    xw_b = (xs * dtwj_e).astype(BF16)
    e_last = jnp.broadcast_to(jnp.exp(acs_t[:, L - 1:L]), (SSM_HEADS, SSM_STATE))

    ii = lax.broadcasted_iota(I32, (L, L), 0)
    jj = lax.broadcasted_iota(I32, (L, L), 1)
    causal = ii >= jj
    first_head = lax.broadcasted_iota(I32, (L, 2 * SSM_HEAD_DIM), 1) < SSM_HEAD_DIM

    y_intra, y_state = [], []
    for g in range(SSM_GROUPS):
        b_g = xbc[:, SSM_INNER + g * SSM_STATE:SSM_INNER + (g + 1) * SSM_STATE].astype(BF16)
        c_off = SSM_INNER + SSM_GROUPS * SSM_STATE
        c_g = xbc[:, c_off + g * SSM_STATE:c_off + (g + 1) * SSM_STATE].astype(BF16)
        scores = _dot_nt(c_g, b_g)
        rows = slice(g * SSM_GROUP_WIDTH, (g + 1) * SSM_GROUP_WIDTH)
        s_g = s_scr[rows, :]
        y_state.append(_dot_nt(c_g, s_g.astype(BF16)))
        for pair in range(SSM_GROUP_WIDTH // (2 * SSM_HEAD_DIM)):
            h0 = g * (SSM_HEADS // SSM_GROUPS) + 2 * pair
            ms = []
            for h in (h0, h0 + 1):
                decay = jnp.where(causal, jnp.exp(acs[:, h:h + 1] - acs_t[h:h + 1, :]), 0.0)
                ms.append(scores * decay)
            m_pair = jnp.concatenate(ms, axis=1).astype(BF16)
            x_pair = xdt_b[:, h0 * SSM_HEAD_DIM:(h0 + 2) * SSM_HEAD_DIM]
            zero = jnp.zeros_like(x_pair)
            x_bd = jnp.concatenate([jnp.where(first_head, x_pair, zero), jnp.where(first_head, zero, x_pair)], axis=0)
            y_intra.append(_dot(m_pair, x_bd))
        heads = range(g * (SSM_HEADS // SSM_GROUPS), (g + 1) * (SSM_HEADS // SSM_GROUPS))
        row_scale = jnp.concatenate(
            [jnp.broadcast_to(e_last[h:h + 1, :], (SSM_HEAD_DIM, SSM_STATE)) for h in heads], axis=0)
        s_scr[rows, :] = s_g * row_scale + _dot_tn(xw_b[:, rows], b_g)

    y = jnp.concatenate(y_intra, axis=1) + jnp.concatenate(y_state, axis=1) * eacs_e + xs * dsk_ref[...]
    yz = y * zs_ref[...].astype(F32)
    normed = []
    for g in range(SSM_GROUPS):
        t = yz[:, g * SSM_GROUP_WIDTH:(g + 1) * SSM_GROUP_WIDTH]
        normed.append(t * lax.rsqrt(jnp.mean(t * t, axis=-1, keepdims=True) + EPS))
    yn_ref[...] = (jnp.concatenate(normed, axis=1) * nw_ref[...]).astype(BF16)

    @pl.when(c == pl.num_programs(1) - 1)
    def _():
        so_ref[0, 0] = s_scr[...]


def _alias_previous(prev, args, in_specs):
    aliases = {}
    for k, p in enumerate(prev):
        if p is not None:
            aliases[len(args)] = k
            args.append(p)
            in_specs.append(pl.BlockSpec(memory_space=pl.ANY))
    return aliases


def _ssd(zs, xbc, dt, a, cw, cb, dsk_e, nw, *, row0, nb, T, L, depth, layer, init, prev):
    n = zs.shape[0]
    nc = T // L
    blk0 = row0 // L
    row = lambda cols: pl.BlockSpec((L, cols), lambda b, c: (blk0 + b * nc + c, 0))
    tril = jnp.tril(jnp.ones((L, L), F32))
    expand = jnp.repeat(jnp.eye(SSM_HEADS, dtype=BF16), SSM_HEAD_DIM, axis=1)
    eye = jnp.eye(SSM_HEADS, dtype=F32)
    args = [zs, xbc, dt, a, cw, cb, dsk_e, nw, tril, expand, eye]
    in_specs = [row(SSM_INNER), row(SSM_CONV_DIM), row(SSM_HEADS), row(SSM_HEADS),
                _full(cw.shape), _full(cb.shape), _full(dsk_e.shape), _full(nw.shape),
                _full(tril.shape), _full(expand.shape), _full(eye.shape)]
    if init is not None:
        args += [init[0], init[1]]
        in_specs += [pl.BlockSpec((1, 1, SSM_CONV - 1, SSM_CONV_DIM), lambda b, c: (layer, b, 0, 0)),
                     pl.BlockSpec((1, 1, SSM_INNER, SSM_STATE), lambda b, c: (layer, b, 0, 0))]
    out_shape = [jax.ShapeDtypeStruct((n, SSM_INNER), BF16),
                 jax.ShapeDtypeStruct((depth, nb, SSM_CONV - 1, SSM_CONV_DIM), F32),
                 jax.ShapeDtypeStruct((depth, nb, SSM_INNER, SSM_STATE), F32)]
    aliases = _alias_previous(prev, args, in_specs)
    return pl.pallas_call(
        functools.partial(_ssd_kernel, L=L, has_init=init is not None),
        grid=(nb, nc),
        in_specs=in_specs,
        out_specs=[row(SSM_INNER),
                   pl.BlockSpec((1, 1, SSM_CONV - 1, SSM_CONV_DIM), lambda b, c: (layer, b, 0, 0)),
                   pl.BlockSpec((1, 1, SSM_INNER, SSM_STATE), lambda b, c: (layer, b, 0, 0))],
        out_shape=out_shape,
        scratch_shapes=[pltpu.VMEM((SSM_INNER, SSM_STATE), F32), pltpu.VMEM((8, SSM_CONV_DIM), F32)],
        input_output_aliases=aliases,
        compiler_params=_params("arbitrary", "arbitrary"),
        name=f"ssd_L{L}",
    )(*args)


def _hgrn_levels(C):
    s, out = C // 2, []
    while s >= HG_BAND:
        out.append(s)
        s //= 2
    return out


def _shift_in_block(x, d):
    if d == 0:
        return x
    n, w = x.shape
    return pltpu.roll(x.reshape(n // HG_BAND, HG_BAND, w), d, axis=1).reshape(n, w)


def _hgrn_kernel(q_ref, k_ref, v_ref, g_ref, og_ref, nw_ref, tril_ref, ones_ref, *rest, C, has_init):
    if has_init:
        s0_ref = rest[0]
    o_ref, so_ref, st_scr, pbuf, rbuf = rest[-5:]
    c = pl.program_id(1)
    K = HG_HEAD_DIM

    @pl.when(c == 0)
    def _():
        for h in range(HG_HEADS):
            st_scr[h] = s0_ref[0, 0, h].T if has_init else jnp.zeros((K, K), F32)

    q = q_ref[...].astype(F32)
    k = k_ref[...].astype(F32)
    v = v_ref[...].astype(F32)
    gcs = _dot_hi(tril_ref[...], g_ref[...])

    row_in_block = lax.broadcasted_iota(I32, (C, HG_DIM), 0) % HG_BAND
    for d in range(HG_BAND):
        ks = _shift_in_block(k, d)
        gs = _shift_in_block(gcs, d)
        e = jnp.where(row_in_block >= d, jnp.exp(gcs - gs), 0.0)
        p = (q * ks * e).astype(BF16)
        for h in range(HG_HEADS):
            r0 = (d * HG_HEADS + h) * C
            pbuf[r0:r0 + C, :] = p[:, h * K:(h + 1) * K]
    rbuf[...] = _dot(pbuf[...], ones_ref[...])

    ii = lax.broadcasted_iota(I32, (C, C), 0)
    jj = lax.broadcasted_iota(I32, (C, C), 1)
    levels = _hgrn_levels(C)
    att = [jnp.zeros((C, C), F32) for _ in range(HG_HEADS)]
    for s in levels:
        q_parts, k_parts = [], []
        zero = jnp.zeros((s, HG_DIM), F32)
        for lo in range(0, C, 2 * s):
            mid = lo + s
            ref_row = gcs[mid - 1:mid, :]
            q_parts += [zero, q[mid:mid + s] * jnp.exp(gcs[mid:mid + s] - ref_row)]
            k_parts += [k[lo:mid] * jnp.exp(ref_row - gcs[lo:mid]), zero]
        qt = jnp.concatenate(q_parts, axis=0).astype(BF16)
        kt = jnp.concatenate(k_parts, axis=0).astype(BF16)
        same_segment = (ii // (2 * s)) == (jj // (2 * s))
        for h in range(HG_HEADS):
            att[h] = att[h] + jnp.where(same_segment, _dot_nt(qt[:, h * K:(h + 1) * K], kt[:, h * K:(h + 1) * K]), 0.0)

    last = gcs[C - 1:C, :]
    q_state = (q * jnp.exp(gcs)).astype(BF16)
    k_last = (k * jnp.exp(last - gcs)).astype(BF16)
    e_last = jnp.exp(last)
    v_b = v_ref[...]
    outs = []
    for h in range(HG_HEADS):
        cols = slice(h * K, (h + 1) * K)
        st = st_scr[h]
        o = _dot_nt(q_state[:, cols], st.astype(BF16))
        if levels:
            o = o + _dot(att[h].astype(BF16), v_b[:, cols])
        for d in range(HG_BAND):
            r0 = (d * HG_HEADS + h) * C
            o = o + rbuf[r0:r0 + C, :] * _shift_in_block(v[:, cols], d)
        st_scr[h] = st * e_last[:, cols] + _dot_tn(v_b[:, cols], k_last[:, cols])
        on = o * lax.rsqrt(jnp.mean(o * o, axis=-1, keepdims=True) + EPS)
        outs.append(on)
    o_all = jnp.concatenate(outs, axis=1) * nw_ref[...] * og_ref[...].astype(F32)
    o_ref[...] = o_all.astype(BF16)

    @pl.when(c == pl.num_programs(1) - 1)
    def _():
        for h in range(HG_HEADS):
            so_ref[0, 0, h] = st_scr[h].T


def _hgrn(qs, kk, v, logf, ogs, nw_e, *, row0, nb, T, C, depth, layer, init, prev):
    n = qs.shape[0]
    nc = T // C
    blk0 = row0 // C
    row = pl.BlockSpec((C, HG_DIM), lambda b, c: (blk0 + b * nc + c, 0))
    K = HG_HEAD_DIM
    tril = jnp.tril(jnp.ones((C, C), F32))
    ones = jnp.ones((K, K), BF16)
    args = [qs, kk, v, logf, ogs, nw_e, tril, ones]
    in_specs = [row, row, row, row, row, _full(nw_e.shape), _full(tril.shape), _full(ones.shape)]
    state_spec = pl.BlockSpec((1, 1, HG_HEADS, K, K), lambda b, c: (layer, b, 0, 0, 0))
    if init is not None:
        args.append(init)
        in_specs.append(state_spec)
    out_shape = [jax.ShapeDtypeStruct((n, HG_DIM), BF16),
                 jax.ShapeDtypeStruct((depth, nb, HG_HEADS, K, K), F32)]
    aliases = _alias_previous(prev, args, in_specs)
    band_rows = HG_BAND * HG_HEADS * C
    return pl.pallas_call(
        functools.partial(_hgrn_kernel, C=C, has_init=init is not None),
        grid=(nb, nc),
        in_specs=in_specs,
        out_specs=[row, state_spec],
        out_shape=out_shape,
        scratch_shapes=[pltpu.VMEM((HG_HEADS, K, K), F32),
                        pltpu.VMEM((band_rows, K), BF16), pltpu.VMEM((band_rows, K), F32)],
        input_output_aliases=aliases,
        compiler_params=_params("arbitrary", "arbitrary"),
        name=f"hgrn_C{C}",
    )(*args)


def _merge_kernel(yn_ref, o_ref, x_ref, sc1_ref, sh1_ref, nw1_ref, wg_ref, g1_ref, sc2_ref, sh2_ref, nw2_ref,
                  wbs_ref, wbh_ref, wout_ref, rwt_ref, rb_ref, before_ref,
                  xo_ref, h2_ref, idx_ref, p_ref, rank_ref, cnt_ref, cnt_scr):
    hb = _mod_norm(x_ref[...], nw1_ref[...], sc1_ref[...], sh1_ref[...]).astype(BF16)
    gate_ssm = _sigmoid(_dot(hb, wg_ref[:, :D_MODEL]))
    gate_hgrn = _sigmoid(_dot(hb, wg_ref[:, D_MODEL:]))
    m = gate_ssm * _dot(yn_ref[...], wbs_ref[...]) + gate_hgrn * _dot(o_ref[...], wbh_ref[...])
    mo = _dot(m.astype(BF16), wout_ref[...])
    x3 = x_ref[...].reshape(ROW_TILE // MOD_ROWS, MOD_ROWS, D_MODEL)
    mo3 = mo.reshape(ROW_TILE // MOD_ROWS, MOD_ROWS, D_MODEL)
    xn = (x3 + g1_ref[...] * mo3).reshape(ROW_TILE, D_MODEL)
    xo_ref[...] = xn
    h2 = _mod_norm(xn, nw2_ref[...], sc2_ref[...], sh2_ref[...])
    _to_token_tiles(h2_ref, 0, h2)
    h2_hi = h2.astype(BF16)
    h2_lo = (h2 - h2_hi.astype(F32)).astype(BF16)
    logits = _dot_nt(rwt_ref[...], h2_hi) + _dot_nt(rwt_ref[...], h2_lo) + rb_ref[...]
    expert = lax.broadcasted_iota(I32, logits.shape, 0)
    vals, idxs, chosen = [], [], []
    for _ in range(TOP_K):
        mx = jnp.max(logits, axis=0, keepdims=True)
        first = jnp.min(jnp.where(logits == mx, expert, N_EXPERTS), axis=0, keepdims=True)
        sel = expert == first
        vals.append(mx)
        idxs.append(first)
        chosen.append(sel)
        logits = jnp.where(sel, -jnp.inf, logits)
    ev = [jnp.exp(v - vals[0]) for v in vals]
    den = ev[0] + ev[1] + ev[2] + ev[3]
    p_ref[...] = jnp.concatenate([e / den for e in ev] + [jnp.zeros((8 - TOP_K, ROW_TILE), F32)], axis=0)
    idx_ref[...] = jnp.concatenate(idxs, axis=0)

    @pl.when(pl.program_id(0) == 0)
    def _():
        cnt_scr[...] = jnp.zeros(cnt_scr.shape, F32)

    member = jnp.zeros(logits.shape, F32)
    for sel in chosen:
        member = member + sel.astype(F32)
    base = cnt_scr[...] + _dot(member.astype(BF16), before_ref[...])
    rank_ref[...] = jnp.concatenate(
        [jnp.sum(jnp.where(sel, base, 0.0), axis=0, keepdims=True) for sel in chosen], axis=0).astype(I32)
    cnt_scr[...] = cnt_scr[...] + jnp.sum(member, axis=1, keepdims=True)
    cnt_ref[...] = cnt_scr[:, :128]


def _merge(yn, o, x, sc1, sh1, nw1, wg, g1, sc2, sh2, nw2, wbs, wbh, wout, rwt, rb):
    n = x.shape[0]
    col_spec = lambda rows: pl.BlockSpec((rows, ROW_TILE), lambda i: (0, i))
    before = jnp.triu(jnp.ones((ROW_TILE, ROW_TILE), BF16), k=1)
    return pl.pallas_call(
        _merge_kernel,
        grid=(n // ROW_TILE,),
        in_specs=[_row_spec(SSM_INNER), _row_spec(HG_DIM), _row_spec(D_MODEL),
                  _mod_spec(), _mod_spec(), _full((1, D_MODEL)), _full(wg.shape),
                  _mod_spec(), _mod_spec(), _mod_spec(), _full((1, D_MODEL)),
                  _full(wbs.shape), _full(wbh.shape), _full(wout.shape), _full(rwt.shape), _full((N_EXPERTS, 1)),
                  _full(before.shape)],
        out_specs=[_row_spec(D_MODEL), pl.BlockSpec((ROW_TILE * TOKEN_TILE_ROWS, 128), lambda i: (i, 0)),
                   col_spec(TOP_K), col_spec(8), col_spec(TOP_K), _full((N_EXPERTS, 128))],
        out_shape=[jax.ShapeDtypeStruct((n, D_MODEL), F32), jax.ShapeDtypeStruct((n * TOKEN_TILE_ROWS, 128), F32),
                   jax.ShapeDtypeStruct((TOP_K, n), I32), jax.ShapeDtypeStruct((8, n), F32),
                   jax.ShapeDtypeStruct((TOP_K, n), I32), jax.ShapeDtypeStruct((N_EXPERTS, 128), F32)],
        scratch_shapes=[pltpu.VMEM((N_EXPERTS, ROW_TILE), F32)],
        compiler_params=_params("arbitrary"),
        name="merge_router",
    )(yn, o, x, sc1, sh1, nw1, wg, g1, sc2, sh2, nw2, wbs, wbh, wout, rwt, rb, before)


TOKEN_TILE_ROWS = D_MODEL // 128


def _to_token_tiles(ref, base, x):
    for c in range(TOKEN_TILE_ROWS):
        ref[pl.ds(base + c, x.shape[0], stride=TOKEN_TILE_ROWS), :] = x[:, c * 128:(c + 1) * 128]


def _from_token_tiles(ref, base, n):
    return jnp.concatenate([ref[pl.ds(base + c, n, stride=TOKEN_TILE_ROWS), :] for c in range(TOKEN_TILE_ROWS)],
                           axis=1)


def _row_copy(src, src_row, dst, dst_row, sem):
    def first(row):
        start = row * TOKEN_TILE_ROWS
        return start if isinstance(start, int) else pl.multiple_of(start, TOKEN_TILE_ROWS)
    return pltpu.make_async_copy(src.at[pl.ds(first(src_row), TOKEN_TILE_ROWS), :],
                                 dst.at[pl.ds(first(dst_row), TOKEN_TILE_ROWS), :], sem)


def _start_rows(idx_ref, copy_of_row):
    def body(r, carry):
        copy_of_row(r, idx_ref[0, 0, r]).start()
        return carry
    lax.fori_loop(0, EXPERT_TILE, body, 0, unroll=8)


def _gmm_kernel(te_ref, src_ref, src_next_ref, dst_ref, dst_prev_ref, h_hbm, wgu_ref, bgu_ref, wd_ref, bd_ref,
                y_hbm, xbuf, ybuf, gsem, ssem, wgu_b, wd_b):
    i = pl.program_id(0)
    last = pl.num_programs(0) - 1
    slot = i % 2
    half_rows = EXPERT_TILE * TOKEN_TILE_ROWS

    def half(buf_ref, buf):
        return buf_ref.at[pl.ds(pl.multiple_of(buf * half_rows, half_rows), half_rows), :]

    def gather(buf):
        return lambda r, row: _row_copy(h_hbm, row, half(xbuf, buf), r, gsem.at[buf])

    def scatter(buf):
        return lambda r, row: _row_copy(half(ybuf, buf), r, y_hbm, row, ssem.at[0])

    def wait_gather(buf):
        pltpu.make_async_copy(h_hbm.at[pl.ds(0, half_rows), :], half(xbuf, buf), gsem.at[buf]).wait()

    def wait_scatter(buf):
        pltpu.make_async_copy(half(ybuf, buf), y_hbm.at[pl.ds(0, half_rows), :], ssem.at[0]).wait()

    @pl.when(i == 0)
    def _():
        ybuf[...] = jnp.zeros(ybuf.shape, F32)
        spare = y_hbm.at[pl.ds(y_hbm.shape[0] - 2 * half_rows, 2 * half_rows), :]
        zero_spare = pltpu.make_async_copy(ybuf, spare, ssem.at[0])
        zero_spare.start()
        zero_spare.wait()
        _start_rows(src_ref, gather(0))

    @pl.when(i > 0)
    def _():
        wait_scatter(slot)

    _start_rows(src_next_ref, gather(1 - slot))
    _start_rows(dst_prev_ref, scatter(1 - slot))

    @pl.when(jnp.logical_or(i == 0, te_ref[i] != te_ref[jnp.maximum(i - 1, 0)]))
    def _():
        wgu_b[...] = wgu_ref[0].astype(BF16)
        wd_b[...] = wd_ref[0].astype(BF16)

    wait_gather(slot)
    base = pl.multiple_of(slot * half_rows, half_rows)
    x = _from_token_tiles(xbuf, base, EXPERT_TILE).astype(BF16)
    gu = _dot(x, wgu_b[...]) + bgu_ref[0]
    gate = jnp.minimum(gu[:, :D_FF], SWIGLU_LIMIT)
    up = jnp.clip(gu[:, D_FF:], -SWIGLU_LIMIT, SWIGLU_LIMIT)
    act = gate * _sigmoid(SWIGLU_ALPHA * gate) * (up + 1.0)
    _to_token_tiles(ybuf, base, _dot(act.astype(BF16), wd_b[...]) + bd_ref[0])

    @pl.when(i == last)
    def _():
        wait_scatter(1 - slot)
        _start_rows(dst_ref, scatter(slot))
        wait_scatter(slot)
        wait_gather(1 - slot)


def _gmm(tile_expert, src, dst_ext, h2, wgu, bgu, wd, bd, layer, n_out_rows):
    n_tiles = tile_expert.shape[0]
    smem_idx = lambda f: pl.BlockSpec((1, 1, EXPERT_TILE), f, memory_space=pltpu.SMEM)
    weight = lambda shape: pl.BlockSpec((1, 1) + shape, lambda i, te: (layer, te[i], 0, 0))
    grid_spec = pltpu.PrefetchScalarGridSpec(
        num_scalar_prefetch=1,
        grid=(n_tiles,),
        in_specs=[smem_idx(lambda i, te: (i, 0, 0)),
                  smem_idx(lambda i, te: (jnp.minimum(i + 1, n_tiles - 1), 0, 0)),
                  smem_idx(lambda i, te: (i + 1, 0, 0)),
                  smem_idx(lambda i, te: (i, 0, 0)),
                  pl.BlockSpec(memory_space=pl.ANY),
                  weight((D_MODEL, 2 * D_FF)), weight((1, 2 * D_FF)), weight((D_FF, D_MODEL)), weight((1, D_MODEL))],
        out_specs=pl.BlockSpec(memory_space=pl.ANY),
        scratch_shapes=[pltpu.VMEM((2 * EXPERT_TILE * TOKEN_TILE_ROWS, 128), F32),
                        pltpu.VMEM((2 * EXPERT_TILE * TOKEN_TILE_ROWS, 128), F32),
                        pltpu.SemaphoreType.DMA((2,)), pltpu.SemaphoreType.DMA((1,)),
                        pltpu.VMEM((D_MODEL, 2 * D_FF), BF16), pltpu.VMEM((D_FF, D_MODEL), BF16)],
    )
    return pl.pallas_call(
        _gmm_kernel_4d,
        grid_spec=grid_spec,
        out_shape=jax.ShapeDtypeStruct((n_out_rows * TOKEN_TILE_ROWS, 128), F32),
        compiler_params=_params("arbitrary"),
        name="moe_gmm",
    )(tile_expert, src, src, dst_ext, dst_ext, h2, wgu, bgu.reshape(bgu.shape[0], bgu.shape[1], 1, -1), wd,
      bd.reshape(bd.shape[0], bd.shape[1], 1, -1))


def _gmm_kernel_4d(te_ref, src_ref, src_next_ref, dst_ref, dst_prev_ref, h_hbm, wgu_ref, bgu_ref, wd_ref, bd_ref,
                   *rest):
    return _gmm_kernel(te_ref, src_ref, src_next_ref, dst_ref, dst_prev_ref, h_hbm, wgu_ref.at[0], bgu_ref.at[0],
                       wd_ref.at[0], bd_ref.at[0], *rest)


def _combine_kernel(y0_ref, y1_ref, y2_ref, y3_ref, p_ref, eye_ref, x_ref, g2_ref, fw_ref, xo_ref, *, final):
    p = _dot_nt(eye_ref[...], p_ref[...], precision=HI)
    acc = jnp.zeros((ROW_TILE, D_MODEL), F32)
    for k, y_ref in enumerate((y0_ref, y1_ref, y2_ref, y3_ref)):
        acc = acc + p[:, k:k + 1] * _from_token_tiles(y_ref, 0, ROW_TILE)
    x3 = x_ref[...].reshape(ROW_TILE // MOD_ROWS, MOD_ROWS, D_MODEL)
    acc3 = acc.reshape(ROW_TILE // MOD_ROWS, MOD_ROWS, D_MODEL)
    xn = (x3 + g2_ref[...] * acc3).reshape(ROW_TILE, D_MODEL)
    if final:
        xn = xn * lax.rsqrt(jnp.mean(xn * xn, axis=-1, keepdims=True) + EPS) * fw_ref[...]
    xo_ref[...] = xn


def _combine(y_pairs, p_t, x, g2, fw, final):
    n = x.shape[0]
    n_tiles = n // ROW_TILE
    eye = jnp.eye(ROW_TILE, dtype=F32)
    y_spec = lambda k: pl.BlockSpec((ROW_TILE * TOKEN_TILE_ROWS, 128), lambda i: (k * n_tiles + i, 0))
    return pl.pallas_call(
        functools.partial(_combine_kernel, final=final),
        grid=(n_tiles,),
        in_specs=[y_spec(0), y_spec(1), y_spec(2), y_spec(3),
                  pl.BlockSpec((8, ROW_TILE), lambda i: (0, i)), _full(eye.shape),
                  _row_spec(D_MODEL), _mod_spec(), _full((1, D_MODEL))],
        out_specs=_row_spec(D_MODEL),
        out_shape=jax.ShapeDtypeStruct((n, D_MODEL), F32),
        compiler_params=_params("arbitrary"),
        name="moe_combine",
    )(y_pairs, y_pairs, y_pairs, y_pairs, p_t, eye, x, g2, fw)


def _route(idx_t, rank_t, counts):
    n = idx_t.shape[1]
    n_pairs = TOP_K * n
    n_tiles = n_pairs // EXPERT_TILE + N_EXPERTS
    counts = counts[:, 0].astype(I32)
    padded = (counts + EXPERT_TILE - 1) // EXPERT_TILE * EXPERT_TILE
    ends = jnp.cumsum(padded)
    starts = ends - padded
    experts = jnp.arange(N_EXPERTS, dtype=I32)
    start_of = jnp.sum(jnp.where(idx_t[..., None] == experts, starts, 0), axis=-1)
    slots = (start_of + rank_t).reshape(-1)
    pair = jnp.full((n_tiles * EXPERT_TILE,), -1, I32).at[slots].set(jnp.arange(n_pairs, dtype=I32))
    row = jnp.arange(n_tiles * EXPERT_TILE, dtype=I32)
    spare = n_pairs + ((row // EXPERT_TILE) % 2) * EXPERT_TILE + row % EXPERT_TILE
    src = jnp.where(pair >= 0, pair % n, 0).reshape(n_tiles, 1, EXPERT_TILE)
    dst = jnp.where(pair >= 0, pair, spare)
    before_first = n_pairs + EXPERT_TILE + jnp.arange(EXPERT_TILE, dtype=I32)
    dst_ext = jnp.concatenate([before_first, dst]).reshape(n_tiles + 1, 1, EXPERT_TILE)
    tile_start = jnp.arange(n_tiles, dtype=I32) * EXPERT_TILE
    te = jnp.minimum(jnp.sum((ends[None, :] <= tile_start[:, None]).astype(I32), axis=1), N_EXPERTS - 1)
    last_used = jnp.max(jnp.where(counts > 0, experts, 0))
    te = jnp.where(tile_start < ends[-1], te, last_used)
    return src, dst_ext, te, n_pairs + 2 * EXPERT_TILE


def kernel(x_prompt, x_sample, c_prompt, c_sample, state_ssm, state_conv, state_hgrn, norm_mix_w, norm_ffn_w, ada_w, ada_b, w_in, conv_w, conv_b, dt_bias, a_log, d_skip, ssm_norm_w, hgrn_lb_logits, hgrn_norm_w, w_branch_ssm, w_branch_hgrn, w_out, router_w, router_b, w_gate_up, b_gate_up, w_down, b_down, final_norm_w):
    nbp, tp, _ = x_prompt.shape
    nbs, ts, _ = x_sample.shape
    depth = w_in.shape[0]
    n_p, n_s = nbp * tp, nbs * ts
    assert depth == 2 and tp % ROW_TILE == 0 and n_s % ROW_TILE == 0 and ts == MOD_ROWS
    l_p, l_s = math.gcd(tp, SSM_CHUNK), math.gcd(ts, SSM_CHUNK)
    c_p, c_s = math.gcd(tp, HG_CHUNK), math.gcd(ts, HG_CHUNK)
    assert c_s == HG_BAND and c_p % HG_BAND == 0

    x = jnp.concatenate([x_prompt.reshape(n_p, D_MODEL), x_sample.reshape(n_s, D_MODEL)], axis=0)
    c_all = jnp.concatenate([c_prompt, c_sample], axis=0)

    lb_all = jnp.cumsum(jax.nn.softmax(hgrn_lb_logits.astype(F32), axis=0), axis=0)
    lb_all = lb_all - lb_all[:1]

    state_ssm2 = state_ssm.reshape(depth, nbs, SSM_INNER, SSM_STATE)
    offs = [0, 2048, 5120, 5152, 6176, 7200, 8224, 9248, 10272, 11296]

    ssd_p = ssd_s = [None, None, None]
    hg_p = hg_s = [None, None]
    for l in range(depth):
        mod = _ada(c_all, ada_w[l], ada_b[l])
        mod_rows = jnp.concatenate([jnp.repeat(mod[:nbp], tp // MOD_ROWS, axis=0), mod[nbp:]], axis=0)
        sh1, sc1, g1, sh2, sc2, g2 = [m.reshape(-1, 1, D_MODEL) for m in jnp.split(mod_rows, 6, axis=-1)]

        wl = w_in[l]
        wz = wl[:, offs[0]:offs[1]].astype(BF16)
        wx = wl[:, offs[1]:offs[2]].astype(BF16)
        wdt = wl[:, offs[2]:offs[3]]
        whg = wl[:, offs[3]:offs[7]].astype(BF16)
        wgates = wl[:, offs[7]:].astype(BF16)
        nw1 = norm_mix_w[l].reshape(1, -1)

        lb = lb_all[l].reshape(1, -1)
        zs, xbc, dt, a, qs, kk, logf, v, ogs = _inproj(
            x, sc1, sh1, nw1, wz, wx, wdt, dt_bias[l].reshape(1, -1), a_log[l].reshape(1, -1),
            whg, jnp.log(lb), jnp.log1p(-lb), 1.0 - lb)

        ssd_args = (zs, xbc, dt, a, conv_w[l], conv_b[l].reshape(1, -1),
                    jnp.repeat(d_skip[l], SSM_HEAD_DIM).reshape(1, -1), ssm_norm_w[l].reshape(1, -1))
        ssd_p = _ssd(*ssd_args, row0=0, nb=nbp, T=tp, L=l_p, depth=depth, layer=l, init=None,
                     prev=[None] + list(ssd_p[1:]))
        ssd_s = _ssd(*ssd_args, row0=n_p, nb=nbs, T=ts, L=l_s, depth=depth, layer=l, init=(state_conv, state_ssm2),
                     prev=[ssd_p[0]] + list(ssd_s[1:]))
        yn = ssd_s[0]

        hg_args = (qs, kk, v, logf, ogs, jnp.tile(hgrn_norm_w[l], HG_HEADS).reshape(1, -1))
        hg_p = _hgrn(*hg_args, row0=0, nb=nbp, T=tp, C=c_p, depth=depth, layer=l, init=None,
                     prev=[None] + list(hg_p[1:]))
        hg_s = _hgrn(*hg_args, row0=n_p, nb=nbs, T=ts, C=c_s, depth=depth, layer=l, init=state_hgrn,
                     prev=[hg_p[0]] + list(hg_s[1:]))
        o = hg_s[0]

        x1, h2, idx_t, p_t, rank_t, counts = _merge(
            yn, o, x, sc1, sh1, nw1, wgates, g1, sc2, sh2, norm_ffn_w[l].reshape(1, -1),
            w_branch_ssm[l].astype(BF16), w_branch_hgrn[l].astype(BF16), w_out[l].astype(BF16),
            router_w[l].T.astype(BF16), router_b[l].reshape(-1, 1))
        src, dst_ext, te, n_out_rows = _route(idx_t, rank_t, counts)
        y_pairs = _gmm(te, src, dst_ext, h2, w_gate_up, b_gate_up, w_down, b_down, l, n_out_rows)
        x = _combine(y_pairs, p_t, x1, g2, final_norm_w.reshape(1, -1), final=(l == depth - 1))

    y_prompt = x[:n_p].reshape(nbp, tp, D_MODEL)
    y_sample = x[n_p:].reshape(nbs, ts, D_MODEL)
    ssm_p = ssd_p[2].reshape(depth, nbp, SSM_HEADS, SSM_HEAD_DIM, SSM_STATE)
    ssm_s = ssd_s[2].reshape(depth, nbs, SSM_HEADS, SSM_HEAD_DIM, SSM_STATE)
    return (y_prompt, y_sample, ssm_p, ssd_p[1], hg_p[1], ssm_s, ssd_s[1], hg_s[1])
```
